```python
import math
import jax
import jax.numpy as jnp
from jax import lax
import numpy as np

D_MODEL = 1024
BATCH = 4
SEQ = 8192
DEPTH = 2

GRID_W = 64
CTX_LEN = 256
HEAD_DIM = 64
GROUP_WIDTH = D_MODEL // 4
MIX_WIDTH = 4 * GROUP_WIDTH
GROUP_HEADS = GROUP_WIDTH // HEAD_DIM
GQA_KV_HEADS = 2
GQA_Q_PER_KV = GROUP_HEADS // GQA_KV_HEADS
DIFF_QK_DIM = HEAD_DIM // 2
HGRN_KEY_DIM = HEAD_DIM
SCAN_CHUNK = 64
Q_BLOCK = 128
ROPE_BASE = 10000.0
N_EXPERTS = 32
N_EXPERT_GROUPS = 8
EXPERTS_PER_GROUP = N_EXPERTS // N_EXPERT_GROUPS
TOP_K = 2
EXPERT_FF = 512
MOE_BLOCK = 256
NORM_EPS = 1e-6
IN_SPLITS = (
    GROUP_WIDTH, GROUP_WIDTH, GROUP_WIDTH, GROUP_WIDTH,
    GROUP_WIDTH, GQA_KV_HEADS * HEAD_DIM, GQA_KV_HEADS * HEAD_DIM,
    GROUP_WIDTH, GROUP_WIDTH, GROUP_WIDTH, GROUP_WIDTH, GROUP_WIDTH,
    GROUP_WIDTH, GROUP_WIDTH, GROUP_WIDTH,
)
IN_WIDTH = sum(IN_SPLITS)

kernel_name = 'hybrid_prefix_diffusion_block'


def rms_norm(x, gain):
    xf = x.astype(jnp.float32)
    y = xf * lax.rsqrt(jnp.mean(xf * xf, axis=-1, keepdims=True) + NORM_EPS)
    return (y * gain.astype(jnp.float32)).astype(x.dtype)


def modulate(x, shift, scale):
    return x * (1 + scale) + shift


def split_heads(a, n_heads):
    return a.reshape(*a.shape[:-1], n_heads, a.shape[-1] // n_heads)


def split_columns(z):
    idx, acc = [], 0
    for w in IN_SPLITS[:-1]:
        acc += w
        idx.append(acc)
    return jnp.split(z, idx, axis=-1)


def axial_rope(rows, rot_dim):
    n_freq = rot_dim // 4
    inv_freq = ROPE_BASE ** (-jnp.arange(n_freq, dtype=jnp.float32) / n_freq)
    row = jnp.broadcast_to(jnp.arange(rows, dtype=jnp.float32)[:, None], (rows, GRID_W)).reshape(-1)
    col = jnp.broadcast_to(jnp.arange(GRID_W, dtype=jnp.float32)[None, :], (rows, GRID_W)).reshape(-1)
    ang = jnp.concatenate([row[:, None] * inv_freq, col[:, None] * inv_freq], axis=-1)
    return jnp.cos(ang), jnp.sin(ang)


def apply_rope(x, cos, sin):
    shape = (cos.shape[0],) + (1,) * (x.ndim - 3) + (cos.shape[-1],)
    cs, sn = cos.reshape(shape), sin.reshape(shape)
    half = x.shape[-1] // 2
    xf = x.astype(jnp.float32)
    x1, x2 = xf[..., :half], xf[..., half:]
    return jnp.concatenate([x1 * cs - x2 * sn, x2 * cs + x1 * sn], axis=-1).astype(x.dtype)


def sweep_query_blocks(fn, q):
    b, t = q.shape[:2]
    nb = t // Q_BLOCK
    qb = jnp.moveaxis(q.reshape(b, nb, Q_BLOCK, *q.shape[2:]), 1, 0)
    ob = lax.map(fn, qb)
    return jnp.moveaxis(ob, 0, 1).reshape(b, t, *ob.shape[3:])


def gqa_attention(q, k, v):
    scale = q.shape[-1] ** -0.5

    def block(qb):
        s = jnp.einsum('bqhgd,bshd->bhgqs', qb, k, preferred_element_type=jnp.float32) * scale
        p = jax.nn.softmax(s, axis=-1).astype(v.dtype)
        return jnp.einsum('bhgqs,bshd->bqhgd', p, v)
    return sweep_query_blocks(block, q)


def differential_attention(q, k, v, lam):
    scale = q.shape[-1] ** -0.5

    def block(qb):
        s = jnp.einsum('bqhmd,bshmd->bhmqs', qb, k, preferred_element_type=jnp.float32) * scale
        p = jax.nn.softmax(s, axis=-1)
        w = (p[:, :, 0] - lam * p[:, :, 1]).astype(v.dtype)
        return jnp.einsum('bhqs,bshd->bqhd', w, v)
    return sweep_query_blocks(block, q)


def linear_recurrence(q, k, v, log_a, s0):
    b, h, t, _ = q.shape
    dv = v.shape[-1]
    n = t // SCAN_CHUNK

    def to_chunks(a):
        return jnp.moveaxis(a.reshape(b, h, n, SCAN_CHUNK, a.shape[-1]), 2, 0)
    causal = jnp.tril(jnp.ones((SCAN_CHUNK, SCAN_CHUNK), bool))[:, :, None]

    def step(state, inp):
        qc, kc, vc, lac = inp
        qf, kf, vf = qc.astype(jnp.float32), kc.astype(jnp.float32), vc.astype(jnp.float32)
        cum = jnp.cumsum(lac.astype(jnp.float32), axis=2)
        o_inter = jnp.einsum('bhtd,bhdv->bhtv', qf * jnp.exp(cum), state)
        rel = cum[:, :, :, None, :] - cum[:, :, None, :, :]
        decay = jnp.where(causal, jnp.exp(jnp.where(causal, rel, 0.0)), 0.0)
        scores = jnp.einsum('bhtd,bhsd,bhtsd->bhts', qf, kf, decay)
        o_intra = jnp.einsum('bhts,bhsv->bhtv', scores, vf)
        last = cum[:, :, -1:, :]
        new_state = jnp.exp(last[:, :, 0, :, None]) * state + jnp.einsum(
            'bhsd,bhsv->bhdv', kf * jnp.exp(last - cum), vf)
        return new_state, o_inter + o_intra

    s_t, o = lax.scan(step, s0.astype(jnp.float32),
                      (to_chunks(q), to_chunks(k), to_chunks(v), to_chunks(log_a)))
    o = jnp.moveaxis(o, 0, 2).reshape(b, h, t, dv)
    return o.astype(v.dtype), s_t


def two_way_prefix_scan(lat, ctx):
    def flip(a):
        return jnp.flip(a, axis=2)
    q, kf, kb, v, af, ab = lat
    qc, kcf, kcb, vc, acf, acb = ctx
    s0 = jnp.zeros((q.shape[0], q.shape[1], q.shape[-1], v.shape[-1]), jnp.float32)
    oc_f, s_f = linear_recurrence(qc, kcf, vc, acf, s0)
    oc_b, s_b = linear_recurrence(flip(qc), flip(kcb), flip(vc), flip(acb), s0)
    o_f, _ = linear_recurrence(q, kf, v, af, s_f)
    o_b, _ = linear_recurrence(flip(q), flip(kb), flip(v), flip(ab), s_b)
    return o_f + flip(o_b), oc_f + flip(oc_b)


def retention_mixer(parts, parts_c, decay_logit, norm_g, rope, need_ctx):
    def prep(p, rotate):
        q, k, v, g = p
        q = split_heads(q, GROUP_HEADS) * HEAD_DIM ** -0.5
        k = split_heads(k, GROUP_HEADS)
        if rotate:
            q, k = apply_rope(q, *rope), apply_rope(k, *rope)
        return (jnp.swapaxes(q, 1, 2), jnp.swapaxes(k, 1, 2),
                jnp.swapaxes(split_heads(v, GROUP_HEADS), 1, 2), g)
    q, k, v, g = prep(parts, True)
    qc, kc, vc, gc = prep(parts_c, False)
    log_gamma = jax.nn.log_sigmoid(decay_logit.astype(jnp.float32))[:, None, :, None, None]

    def log_decay(a, direction):
        return jnp.broadcast_to(log_gamma[direction], a.shape)
    o, oc = two_way_prefix_scan((q, k, k, v, log_decay(k, 0), log_decay(k, 1)),
                                (qc, kc, kc, vc, log_decay(kc, 0), log_decay(kc, 1)))

    def readout(o_, g_):
        return rms_norm(jnp.swapaxes(o_, 1, 2), norm_g).reshape(g_.shape) * jax.nn.silu(g_)
    return readout(o, g), (readout(oc, gc) if need_ctx else None)


def hgrn2_mixer(parts, parts_c, lower_bound, norm_g, need_ctx):
    lb = lower_bound.astype(jnp.float32).reshape(2, GROUP_HEADS, 1, HGRN_KEY_DIM)

    def bhtd(a):
        return jnp.swapaxes(split_heads(a, GROUP_HEADS), 1, 2)

    def gate(z, d):
        z = bhtd(z).astype(jnp.float32)
        f = lb[d] + (1 - lb[d]) * jax.nn.sigmoid(z)
        return (1 - lb[d]) * jax.nn.sigmoid(-z), jnp.log(f)

    def prep(p):
        q, zf, zb, i, g = p
        k_f, lf_f = gate(zf, 0)
        k_b, lf_b = gate(zb, 1)
        return (bhtd(q), k_f, k_b, bhtd(i), lf_f, lf_b), g
    lat, g = prep(parts)
    ctx, gc = prep(parts_c)
    o, oc = two_way_prefix_scan(lat, ctx)

    def readout(o_, g_):
        return rms_norm(jnp.swapaxes(o_, 1, 2), norm_g).reshape(g_.shape) * jax.nn.silu(g_)
    return readout(o, g), (readout(oc, gc) if need_ctx else None)


def gqa_mixer(parts, parts_c, qn_g, kn_g, rope, need_ctx):
    def prep(p, rotate):
        q, k, v = p
        q = rms_norm(split_heads(q, GROUP_HEADS), qn_g)
        k = rms_norm(split_heads(k, GQA_KV_HEADS), kn_g)
        if rotate:
            q, k = apply_rope(q, *rope), apply_rope(k, *rope)
        q = q.reshape(*q.shape[:2], GQA_KV_HEADS, GQA_Q_PER_KV, HEAD_DIM)
        return q, k, split_heads(v, GQA_KV_HEADS)
    q, k, v = prep(parts, True)
    qc, kc, vc = prep(parts_c, False)
    k_all = jnp.concatenate([kc, k], axis=1)
    v_all = jnp.concatenate([vc, v], axis=1)

    def flat(o_):
        return o_.reshape(*o_.shape[:2], GROUP_WIDTH)
    o = flat(gqa_attention(q, k_all, v_all))
    return o, (flat(gqa_attention(qc, kc, vc)) if need_ctx else None)


def diff_mixer(parts, parts_c, lam_params, norm_g, lambda_init, rope, need_ctx):
    def prep(p, rotate):
        q, k, v = p
        q = q.reshape(*q.shape[:-1], GROUP_HEADS, 2, DIFF_QK_DIM)
        k = k.reshape(*k.shape[:-1], GROUP_HEADS, 2, DIFF_QK_DIM)
        if rotate:
            q, k = apply_rope(q, *rope), apply_rope(k, *rope)
        return q, k, split_heads(v, GROUP_HEADS)
    lp = lam_params.astype(jnp.float32)
    lam = jnp.exp(jnp.sum(lp[0] * lp[1])) - jnp.exp(jnp.sum(lp[2] * lp[3])) + lambda_init
    q, k, v = prep(parts, True)
    qc, kc, vc = prep(parts_c, False)
    k_all = jnp.concatenate([kc, k], axis=1)
    v_all = jnp.concatenate([vc, v], axis=1)

    def readout(o_):
        return (rms_norm(o_, norm_g) * (1 - lambda_init)).reshape(*o_.shape[:2], GROUP_WIDTH)
    o = readout(differential_attention(q, k_all, v_all, lam))
    return o, (readout(differential_attention(qc, kc, vc, lam)) if need_ctx else None)


def routed_moe(h, router_w, router_bias, w_gate, w_up, w_down):
    n, d = h.shape
    scores = jax.nn.sigmoid(jnp.dot(h, router_w, preferred_element_type=jnp.float32))
    sel = (scores + router_bias.astype(jnp.float32)).reshape(n, N_EXPERT_GROUPS, EXPERTS_PER_GROUP)
    group_score = lax.top_k(sel, 2)[0].sum(-1)
    best_group = jnp.argmax(group_score, axis=-1)
    in_group = sel[jnp.arange(n), best_group]
    _, local = lax.top_k(in_group, TOP_K)
    expert = best_group[:, None] * EXPERTS_PER_GROUP + local
    gate = jnp.take_along_axis(scores, expert, axis=1)
    gate = gate / jnp.sum(gate, axis=-1, keepdims=True)

    m = n * TOP_K
    e_flat = expert.reshape(m)
    tok = jnp.repeat(jnp.arange(n), TOP_K)
    order = jnp.argsort(e_flat)
    e_s, tok_s, w_s = e_flat[order], tok[order], gate.reshape(m)[order]
    counts = jnp.bincount(e_flat, length=N_EXPERTS)
    padded = (counts + MOE_BLOCK - 1) // MOE_BLOCK * MOE_BLOCK
    start = jnp.cumsum(counts) - counts
    pend = jnp.cumsum(padded)
    pstart = pend - padded
    dest = pstart[e_s] + jnp.arange(m) - start[e_s]
    n_blocks = -(-m // MOE_BLOCK) + N_EXPERTS
    rows = n_blocks * MOE_BLOCK
    xs = jnp.zeros((rows, d), h.dtype).at[dest].set(h[tok_s])
    block_expert = jnp.clip(jnp.searchsorted(pend, jnp.arange(n_blocks) * MOE_BLOCK, side='right'),
                            0, N_EXPERTS - 1)

    def expert_block(args):
        xb, e = args
        a = jnp.dot(xb, w_gate[e])
        u = jnp.dot(xb, w_up[e])
        return jnp.dot(jax.nn.silu(a) * u, w_down[e])
    ys = lax.map(expert_block, (xs.reshape(n_blocks, MOE_BLOCK, d), block_expert)).reshape(rows, d)
    y = jnp.zeros((n, d), jnp.float32).at[tok_s].add(ys[dest].astype(jnp.float32) * w_s[:, None])
    return y.astype(h.dtype)


def trunk_layer(layer, x, xc, c, c_ctx, rope_hd, rope_diff, w_mod, b_mod, norm1_g, norm2_g,
                w_in, ret_decay_logit, ret_norm_g, gqa_qnorm_g, gqa_knorm_g, hgrn_lower_bound,
                hgrn_norm_g, diff_lambda, diff_norm_g, w_out, router_w, router_bias,
                moe_w_gate, moe_w_up, moe_w_down, need_ctx):
    b, t, d = x.shape
    mod = jnp.dot(jax.nn.silu(c), w_mod) + b_mod
    mod_c = jnp.dot(jax.nn.silu(c_ctx), w_mod) + b_mod
    sh1, sc1, g1, sh2, sc2, g2 = jnp.split(mod[:, None, :], 6, axis=-1)
    sh1c, sc1c, g1c, sh2c, sc2c, g2c = jnp.split(mod_c, 6, axis=-1)

    h = modulate(rms_norm(x, norm1_g), sh1, sc1)
    hc = modulate(rms_norm(xc, norm1_g), sh1c, sc1c)
    p = split_columns(jnp.dot(h, w_in))
    pc = split_columns(jnp.dot(hc, w_in))
    lambda_init = 0.8 - 0.6 * math.exp(-0.3 * layer)
    o_ret, oc_ret = retention_mixer(p[0:4], pc[0:4], ret_decay_logit, ret_norm_g, rope_hd, need_ctx)
    o_gqa, oc_gqa = gqa_mixer(p[4:7], pc[4:7], gqa_qnorm_g, gqa_knorm_g, rope_hd, need_ctx)
    o_hg, oc_hg = hgrn2_mixer(p[7:12], pc[7:12], hgrn_lower_bound, hgrn_norm_g, need_ctx)
    o_df, oc_df = diff_mixer(p[12:15], pc[12:15], diff_lambda, diff_norm_g, lambda_init,
                             rope_diff, need_ctx)
    x = x + g1 * jnp.dot(jnp.concatenate([o_ret, o_gqa, o_hg, o_df], axis=-1), w_out)
    h2 = modulate(rms_norm(x, norm2_g), sh2, sc2).reshape(b * t, d)
    if not need_ctx:
        y = routed_moe(h2, router_w, router_bias, moe_w_gate, moe_w_up, moe_w_down)
        return x + g2 * y.reshape(b, t, d), xc
    xc = xc + g1c * jnp.dot(jnp.concatenate([oc_ret, oc_gqa, oc_hg, oc_df], axis=-1), w_out)
    h2c = modulate(rms_norm(xc, norm2_g), sh2c, sc2c).reshape(-1, d)
    y = routed_moe(jnp.concatenate([h2, h2c], axis=0), router_w, router_bias,
                   moe_w_gate, moe_w_up, moe_w_down)
    x = x + g2 * y[:b * t].reshape(b, t, d)
    xc = xc + g2c * y[b * t:].reshape(xc.shape)
    return x, xc


def setup_inputs(seed: int = 0) -> dict:
    key = jax.random.key(seed)
    ks = jax.random.split(key, 24)

    def nrm(k, shape, s):
        return jax.random.normal(k, shape, jnp.float32) * s
    gamma = 1.0 - jnp.exp2(-5.0 - jnp.arange(GROUP_HEADS, dtype=jnp.float32))
    ret_logit = jnp.log(gamma) - jnp.log1p(-gamma)
    return {
        'x': nrm(ks[0], (BATCH, SEQ, D_MODEL), 1.0),
        'c': nrm(ks[1], (BATCH, D_MODEL), 1.0),
        'ctx': nrm(ks[2], (BATCH, CTX_LEN, D_MODEL), 1.0),
        'c_ctx': nrm(ks[3], (D_MODEL,), 1.0),
        'w_mod': nrm(ks[4], (DEPTH, D_MODEL, 6 * D_MODEL), 0.5 * D_MODEL ** -0.5),
        'b_mod': nrm(ks[5], (DEPTH, 6 * D_MODEL), 0.01),
        'norm1_g': 1.0 + nrm(ks[6], (DEPTH, D_MODEL), 0.02),
        'norm2_g': 1.0 + nrm(ks[7], (DEPTH, D_MODEL), 0.02),
        'w_in': nrm(ks[8], (DEPTH, D_MODEL, IN_WIDTH), D_MODEL ** -0.5),
        'ret_decay_logit': ret_logit + nrm(ks[9], (DEPTH, 2, GROUP_HEADS), 0.1),
        'ret_norm_g': 1.0 + nrm(ks[10], (DEPTH, HEAD_DIM), 0.02),
        'gqa_qnorm_g': 1.0 + nrm(ks[11], (DEPTH, HEAD_DIM), 0.02),
        'gqa_knorm_g': 1.0 + nrm(ks[12], (DEPTH, HEAD_DIM), 0.02),
        'hgrn_lb_logit': nrm(ks[13], (2, DEPTH, GROUP_WIDTH), 0.5),
        'hgrn_norm_g': 1.0 + nrm(ks[14], (DEPTH, HEAD_DIM), 0.02),
        'diff_lambda': nrm(ks[15], (DEPTH, 4, DIFF_QK_DIM), 0.1),
        'diff_norm_g': 1.0 + nrm(ks[16], (DEPTH, HEAD_DIM), 0.02),
        'w_out': nrm(ks[17], (DEPTH, MIX_WIDTH, D_MODEL), MIX_WIDTH ** -0.5),
        'router_w': nrm(ks[18], (D_MODEL, N_EXPERTS), D_MODEL ** -0.5),
        'router_bias': nrm(ks[19], (N_EXPERTS,), 0.01),
        'moe_w_gate': nrm(ks[20], (DEPTH, N_EXPERTS, D_MODEL, EXPERT_FF), D_MODEL ** -0.5),
        'moe_w_up': nrm(ks[21], (DEPTH, N_EXPERTS, D_MODEL, EXPERT_FF), D_MODEL ** -0.5),
        'moe_w_down': nrm(ks[22], (DEPTH, N_EXPERTS, EXPERT_FF, D_MODEL), EXPERT_FF ** -0.5),
        'final_norm_g': 1.0 + nrm(ks[23], (D_MODEL,), 0.02),
    }


def reference(x, c, ctx, c_ctx, w_mod, b_mod, norm1_g, norm2_g, w_in, ret_decay_logit,
              ret_norm_g, gqa_qnorm_g, gqa_knorm_g, hgrn_lb_logit, hgrn_norm_g, diff_lambda,
              diff_norm_g, w_out, router_w, router_bias, moe_w_gate, moe_w_up, moe_w_down,
              final_norm_g):
    n_tokens = x.shape[1]
    rows = n_tokens // GRID_W
    rope_hd = axial_rope(rows, HEAD_DIM)
    rope_diff = axial_rope(rows, DIFF_QK_DIM)
    sm = jax.nn.softmax(hgrn_lb_logit.astype(jnp.float32), axis=1)
    lower_bounds = jnp.cumsum(sm, axis=1) - sm[:, :1]
    xc = ctx
    for layer in range(DEPTH):
        x, xc = trunk_layer(
            layer, x, xc, c, c_ctx, rope_hd, rope_diff, w_mod[layer], b_mod[layer],
            norm1_g[layer], norm2_g[layer], w_in[layer], ret_decay_logit[layer],
            ret_norm_g[layer], gqa_qnorm_g[layer], gqa_knorm_g[layer], lower_bounds[:, layer],
            hgrn_norm_g[layer], diff_lambda[layer], diff_norm_g[layer], w_out[layer],
            router_w, router_bias, moe_w_gate[layer], moe_w_up[layer], moe_w_down[layer],
            layer < DEPTH - 1)
    return rms_norm(x, final_norm_g)
```

```python
import functools
import math

import jax
import jax.numpy as jnp
from jax import lax
from jax.experimental import pallas as pl
from jax.experimental.pallas import tpu as pltpu

F32 = jnp.float32
BF16 = jnp.bfloat16
HIGHEST = lax.Precision.HIGHEST

HEAD_DIM = 64
GROUP_HEADS = 4
GROUP_WIDTH = GROUP_HEADS * HEAD_DIM
GQA_KV_HEADS = 2
DIFF_QK_DIM = HEAD_DIM // 2
GRID_W = 64
ROPE_BASE = 10000.0
N_EXPERTS = 32
N_EXPERT_GROUPS = 8
EXPERTS_PER_GROUP = 4
MOE_BLOCK = 256
NORM_EPS = 1e-6

ROW_TILE = 256
REC_CHUNK = 128
REC_SUB = 16
ATT_Q_TILE = 256
ATT_KV_CHUNK = 1024
VMEM_LIMIT = 56 * 1024 * 1024

NT_DIMS = (((1,), (1,)), ((), ()))


def _sigmoid(v):
    return 1.0 / (1.0 + jnp.exp(-v))


def _cparams(*sem):
    return pltpu.CompilerParams(dimension_semantics=sem, vmem_limit_bytes=VMEM_LIMIT)


def _mod_kernel(c_ref, w_ref, b_ref, o_ref):
    cv = c_ref[...]
    s = cv * _sigmoid(cv)
    o_ref[...] = jnp.dot(s, w_ref[...], precision=HIGHEST, preferred_element_type=F32) + b_ref[...]


def _mod_call(cvec, w_mod, b_mod):
    d, n = w_mod.shape
    tn = 1024
    return pl.pallas_call(
        _mod_kernel,
        grid=(n // tn,),
        in_specs=[pl.BlockSpec((8, d), lambda j: (0, 0)),
                  pl.BlockSpec((d, tn), lambda j: (0, j)),
                  pl.BlockSpec((1, tn), lambda j: (0, j))],
        out_specs=pl.BlockSpec((8, tn), lambda j: (0, j)),
        out_shape=jax.ShapeDtypeStruct((8, n), F32),
        compiler_params=_cparams("arbitrary"),
        name="mod",
    )(cvec, w_mod, b_mod.reshape(1, n))


def _rope(v, cos, sin_a, sin_b, half):
    n = v.shape[-1]
    return v * cos + pltpu.roll(v, n - half, 1) * sin_a + pltpu.roll(v, half, 1) * sin_b


def _in_proj_kernel(x_ref, mod_ref, g1_ref, w_ref, wvt_ref, c64_ref, sa64_ref, sb64_ref,
                    c32_ref, sa32_ref, sb32_ref, qg_ref, kg_ref, seg_ref,
                    zr_ref, zg_ref, zh_ref, zd_ref, vt_ref):
    d = x_ref.shape[-1]
    gw = GROUP_WIDTH
    x = x_ref[...]
    ms = jnp.mean(x * x, axis=-1, keepdims=True)
    h = x * lax.rsqrt(ms + NORM_EPS) * g1_ref[...]
    mod = mod_ref[0]
    h = h * (1.0 + mod[:, d:2 * d]) + mod[:, 0:d]
    hb = h.astype(BF16)

    def proj(c0, c1):
        return jnp.dot(hb, w_ref[:, c0:c1], preferred_element_type=F32)

    c64, sa64, sb64 = c64_ref[...], sa64_ref[...], sb64_ref[...]
    c32, sa32, sb32 = c32_ref[...], sa32_ref[...], sb32_ref[...]
    seg = seg_ref[...]

    zr_ref[:, 0:gw] = _rope(proj(0, gw) * HEAD_DIM ** -0.5, c64, sa64, sb64, 32)
    zr_ref[:, gw:2 * gw] = _rope(proj(gw, 2 * gw), c64, sa64, sb64, 32)
    zr_ref[:, 2 * gw:4 * gw] = proj(2 * gw, 4 * gw)

    o = 4 * gw
    q = proj(o, o + gw)
    qms = jnp.dot(q * q, seg, precision=HIGHEST, preferred_element_type=F32)
    q = q * lax.rsqrt(qms + NORM_EPS) * qg_ref[...]
    q = _rope(q, c64, sa64, sb64, 32) * HEAD_DIM ** -0.5
    lane = lax.broadcasted_iota(jnp.int32, (q.shape[0], 128), 1)
    lo = lane < HEAD_DIM
    qa, qb = q[:, 0:128], q[:, 128:256]
    zg_ref[:, 0:128] = jnp.where(lo, qa, 0.0).astype(BF16)
    zg_ref[:, 128:256] = jnp.where(lo, pltpu.roll(qa, 64, 1), 0.0).astype(BF16)
    zg_ref[:, 256:384] = jnp.where(lo, 0.0, pltpu.roll(qb, 64, 1)).astype(BF16)
    zg_ref[:, 384:512] = jnp.where(lo, 0.0, qb).astype(BF16)
    k = proj(o + gw, o + gw + 128)
    kms = jnp.dot(k * k, seg[0:128, 0:128], precision=HIGHEST, preferred_element_type=F32)
    k = k * lax.rsqrt(kms + NORM_EPS) * kg_ref[...]
    zg_ref[:, 512:640] = _rope(k, c64[:, 0:128], sa64[:, 0:128], sb64[:, 0:128], 32).astype(BF16)
    zg_ref[:, 640:768] = proj(o + gw + 128, o + gw + 256).astype(BF16)

    o = 6 * gw
    zh_ref[...] = proj(o, o + 5 * gw)

    o = 11 * gw
    zd_ref[:, 0:gw] = (_rope(proj(o, o + gw), c32, sa32, sb32, 16) * DIFF_QK_DIM ** -0.5).astype(BF16)
    zd_ref[:, gw:2 * gw] = _rope(proj(o + gw, o + 2 * gw), c32, sa32, sb32, 16).astype(BF16)
    zd_ref[:, 2 * gw:3 * gw] = proj(o + 2 * gw, o + 3 * gw).astype(BF16)

    vt_ref[...] = lax.dot_general(wvt_ref[...], hb, NT_DIMS, preferred_element_type=F32).astype(BF16)


def _in_proj_call(x_all, mod_all, g1, w_bf, wvt_bf, tabs, qg, kg, seg, n_lat_rows, t_len):
    r, d = x_all.shape
    tm = ROW_TILE
    gw = GROUP_WIDTH
    n_lat_tiles = n_lat_rows // tm
    tiles_per_seq = t_len // tm
    n_batch = n_lat_rows // t_len

    def mod_idx(i):
        return (jnp.where(i < n_lat_tiles, i // tiles_per_seq, n_batch), 0, 0)

    def tab_idx(i):
        return (jnp.where(i < n_lat_tiles, i % tiles_per_seq, tiles_per_seq), 0)

    row = lambda i: (i, 0)
    const = lambda i: (0, 0)
    tab_spec = pl.BlockSpec((tm, gw), tab_idx)
    return pl.pallas_call(
        _in_proj_kernel,
        grid=(r // tm,),
        in_specs=[pl.BlockSpec((tm, d), row),
                  pl.BlockSpec((1, 1, mod_all.shape[-1]), mod_idx),
                  pl.BlockSpec((1, d), const),
                  pl.BlockSpec(w_bf.shape, const),
                  pl.BlockSpec(wvt_bf.shape, const),
                  tab_spec, tab_spec, tab_spec, tab_spec, tab_spec, tab_spec,
                  pl.BlockSpec((1, gw), const),
                  pl.BlockSpec((1, 128), const),
                  pl.BlockSpec((gw, gw), const)],
        out_specs=[pl.BlockSpec((tm, 4 * gw), row),
                   pl.BlockSpec((tm, 3 * gw), row),
                   pl.BlockSpec((tm, 5 * gw), row),
                   pl.BlockSpec((tm, 3 * gw), row),
                   pl.BlockSpec((2 * gw, tm), lambda i: (0, i))],
        out_shape=[jax.ShapeDtypeStruct((r, 4 * gw), F32),
                   jax.ShapeDtypeStruct((r, 3 * gw), BF16),
                   jax.ShapeDtypeStruct((r, 5 * gw), F32),
                   jax.ShapeDtypeStruct((r, 3 * gw), BF16),
                   jax.ShapeDtypeStruct((2 * gw, r), BF16)],
        compiler_params=_cparams("arbitrary"),
        name="in_proj",
    )(x_all, mod_all, g1, w_bf, wvt_bf, *tabs, qg, kg, seg)


def _rec_direction(q_ref, k_ref, v_ref, vt_ref, par, bd_ref, o_ref, s_ref,
                   lc_ref, qt_ref, kh_ref, kk_ref, tot_ref, a_ref, *, hgrn, reverse):
    c, gw = q_ref.shape
    sub = REC_SUB
    n_sub = c // sub
    kin = k_ref[...]
    if hgrn:
        sig = _sigmoid(kin)
        logf = jnp.log(par + (1.0 - par) * sig)
        kk = (1.0 - par) * (1.0 - sig)
    else:
        logf = jnp.broadcast_to(par, (c, gw))
        kk = kin
    r_i = lax.broadcasted_iota(jnp.int32, (c, c), 0)
    c_i = lax.broadcasted_iota(jnp.int32, (c, c), 1)
    shift = sub.bit_length() - 1
    same = jnp.right_shift(r_i, shift) == jnp.right_shift(c_i, shift)
    if reverse:
        incl = same & (c_i >= r_i)
        excl = same & (c_i < r_i)
    else:
        incl = same & (c_i <= r_i)
        excl = same & (c_i > r_i)
    lc = jnp.dot(jnp.where(incl, 1.0, 0.0), logf, precision=HIGHEST, preferred_element_type=F32)
    rr = jnp.dot(jnp.where(excl, 1.0, 0.0), logf, precision=HIGHEST, preferred_element_type=F32)
    lc_ref[...] = lc
    qt_ref[...] = (q_ref[...] * jnp.exp(lc)).astype(BF16)
    kh_ref[...] = kk * jnp.exp(rr)
    kk_ref[...] = kk
    tot_ref[...] = lc + rr
    vt = vt_ref[...]
    bd = bd_ref[...]
    t_loc = lax.broadcasted_iota(jnp.int32, (sub, gw), 0)
    row_id = lax.broadcasted_iota(jnp.int32, (c, gw), 0)
    head_shift = HEAD_DIM.bit_length() - 1
    same_head = (jnp.right_shift(lax.broadcasted_iota(jnp.int32, (gw, gw), 0), head_shift)
                 == jnp.right_shift(lax.broadcasted_iota(jnp.int32, (gw, gw), 1), head_shift))

    def body(j, carry):
        a = (n_sub - 1 - j) if reverse else j
        base = pl.multiple_of(a * sub, sub)
        rows = pl.ds(base, sub)
        s_t = s_ref[...]
        o_inter = lax.dot_general(qt_ref[rows, :], s_t.astype(BF16), NT_DIMS,
                                  preferred_element_type=F32)
        lc_a = lc_ref[rows, :]
        q_a = q_ref[rows, :]
        k_a = kk_ref[rows, :]
        v_a = v_ref[rows, :]
        for s in range(sub):
            valid = (t_loc <= s) if reverse else (t_loc >= s)
            e = jnp.where(valid, jnp.exp(lc_a - lc_a[s:s + 1, :]), 0.0)
            a_ref[s * sub:(s + 1) * sub, :] = (q_a * e * k_a[s:s + 1, :]).astype(BF16)
        b = jnp.dot(a_ref[...], bd, preferred_element_type=F32)
        o_intra = jnp.zeros((sub, gw), F32)
        for s in range(sub):
            o_intra = o_intra + b[s * sub:(s + 1) * sub, :] * v_a[s:s + 1, :]
        o_ref[rows, :] = o_inter + o_intra
        in_sub = (row_id >= base) & (row_id < base + sub)
        kh_m = jnp.where(in_sub, kh_ref[...], 0.0).astype(BF16)
        u = jnp.dot(vt, kh_m, preferred_element_type=F32)
        decay = jnp.exp(tot_ref[pl.ds(base, 1), :])
        s_ref[...] = s_t * decay + jnp.where(same_head, u, 0.0)
        return carry

    lax.fori_loop(0, n_sub, body, 0)


def _rec_kernel(qf_ref, kf_ref, vf_ref, vtf_ref, qb_ref, kb_ref, vb_ref, vtb_ref, par_ref, bd_ref,
                of_ref, ob_ref, sf_ref, sb_ref, lc_ref, qt_ref, kh_ref, kk_ref, tot_ref, a_ref,
                *, hgrn):
    @pl.when(pl.program_id(1) == 0)
    def _():
        sf_ref[...] = jnp.zeros_like(sf_ref)
        sb_ref[...] = jnp.zeros_like(sb_ref)

    tmp = (lc_ref, qt_ref, kh_ref, kk_ref, tot_ref, a_ref)
    _rec_direction(qf_ref, kf_ref, vf_ref, vtf_ref, par_ref[0:1, :], bd_ref, of_ref, sf_ref, *tmp,
                   hgrn=hgrn, reverse=False)
    _rec_direction(qb_ref, kb_ref, vb_ref, vtb_ref, par_ref[1:2, :], bd_ref, ob_ref, sb_ref, *tmp,
                   hgrn=hgrn, reverse=True)


def _rec_call(z, vt_all, par, bd, *, hgrn, n_batch, t_len, ctx_len, cols, vt_row):
    r = z.shape[0]
    c = REC_CHUNK
    gw = GROUP_WIDTH
    n_ctx = ctx_len // c
    n_lat = t_len // c
    lat_blocks = n_batch * n_lat

    def fwd_blk(b, i):
        return jnp.where(i < n_ctx, lat_blocks + b * n_ctx + i, b * n_lat + i - n_ctx)

    def bwd_blk(b, i):
        return jnp.where(i < n_ctx, lat_blocks + b * n_ctx + (n_ctx - 1 - i),
                         b * n_lat + (n_lat - 1 - (i - n_ctx)))

    def zspec(blk, col):
        return pl.BlockSpec((c, gw), lambda b, i: (blk(b, i), col))

    def vtspec(blk):
        return pl.BlockSpec((gw, c), lambda b, i: (vt_row, blk(b, i)))

    const = lambda b, i: (0, 0)
    cq, ckf, ckb, cv = cols
    out_f, out_b = pl.pallas_call(
        functools.partial(_rec_kernel, hgrn=hgrn),
        grid=(n_batch, n_ctx + n_lat),
        in_specs=[zspec(fwd_blk, cq), zspec(fwd_blk, ckf), zspec(fwd_blk, cv), vtspec(fwd_blk),
                  zspec(bwd_blk, cq), zspec(bwd_blk, ckb), zspec(bwd_blk, cv), vtspec(bwd_blk),
                  pl.BlockSpec((2, gw), const),
                  pl.BlockSpec((gw, gw), const)],
        out_specs=[pl.BlockSpec((c, gw), lambda b, i: (fwd_blk(b, i), 0)),
                   pl.BlockSpec((c, gw), lambda b, i: (bwd_blk(b, i), 0))],
        out_shape=[jax.ShapeDtypeStruct((r, gw), F32), jax.ShapeDtypeStruct((r, gw), F32)],
        scratch_shapes=[pltpu.VMEM((gw, gw), F32), pltpu.VMEM((gw, gw), F32),
                        pltpu.VMEM((c, gw), F32), pltpu.VMEM((c, gw), BF16),
                        pltpu.VMEM((c, gw), F32), pltpu.VMEM((c, gw), F32),
                        pltpu.VMEM((c, gw), F32), pltpu.VMEM((REC_SUB * REC_SUB, gw), BF16)],
        compiler_params=_cparams("arbitrary", "arbitrary"),
        name="hgrn_rec" if hgrn else "ret_rec",
    )(z, z, z, vt_all, z, z, z, vt_all, par, bd)
    return out_f, out_b


def _online_softmax(lhs_ref, segments, kv_chunk, m_ref, l_ref, acc_ref):
    m_ref[...] = jnp.full(m_ref.shape, -jnp.inf, F32)
    l_ref[...] = jnp.zeros(l_ref.shape, F32)
    acc_ref[...] = jnp.zeros(acc_ref.shape, F32)

    def update(k_blk, v_blk):
        s = lax.dot_general(lhs_ref[...], k_blk, NT_DIMS, preferred_element_type=F32)
        m = m_ref[...]
        m_new = jnp.maximum(m, jnp.max(s, axis=-1, keepdims=True))
        alpha = jnp.exp(m - m_new)
        p = jnp.exp(s - m_new)
        m_ref[...] = m_new
        l_ref[...] = alpha * l_ref[...] + jnp.sum(p, axis=-1, keepdims=True)
        acc_ref[...] = alpha * acc_ref[...] + jnp.dot(p.astype(BF16), v_blk,
                                                      preferred_element_type=F32)

    for k_ref, v_ref in segments:
        n = k_ref.shape[0]
        if n <= kv_chunk:
            update(k_ref[...], v_ref[...])
        else:
            def body(j, cr, k_ref=k_ref, v_ref=v_ref):
                rows = pl.ds(pl.multiple_of(j * kv_chunk, kv_chunk), kv_chunk)
                update(k_ref[rows, :], v_ref[rows, :])
                return cr
            lax.fori_loop(0, n // kv_chunk, body, 0)
    return acc_ref[...] / l_ref[...]


def _gqa_kernel(q_ref, *refs, n_seg):
    o_ref, lhs_ref, m_ref, l_ref, acc_ref = refs[2 * n_seg:]
    segments = [(refs[2 * i], refs[2 * i + 1]) for i in range(n_seg)]
    tq = q_ref.shape[0]
    lane = lax.broadcasted_iota(jnp.int32, (tq, 128), 1)
    lo = lane < HEAD_DIM
    outs = []
    for j in range(GQA_KV_HEADS):
        lhs_ref[0:tq, :] = q_ref[:, (2 * j) * 128:(2 * j + 1) * 128]
        lhs_ref[tq:2 * tq, :] = q_ref[:, (2 * j + 1) * 128:(2 * j + 2) * 128]
        o = _online_softmax(lhs_ref, segments, ATT_KV_CHUNK, m_ref, l_ref, acc_ref)
        outs.append((o[0:tq], o[tq:2 * tq]))
    (h0, h1), (h2, h3) = outs
    o_ref[:, 0:128] = jnp.where(lo, h0, pltpu.roll(h1, 64, 1))
    o_ref[:, 128:256] = jnp.where(lo, pltpu.roll(h2, 64, 1), h3)


def _diff_kernel(lam_ref, q_ref, *refs, n_seg, out_scale):
    ng_ref, seg_ref, o_ref, lhs_ref, m_ref, l_ref, acc_ref = refs[2 * n_seg:]
    segments = [(refs[2 * i], refs[2 * i + 1]) for i in range(n_seg)]
    tq, gw = q_ref.shape
    lane = lax.broadcasted_iota(jnp.int32, (tq, gw), 1)
    q = q_ref[...].astype(F32)
    lam = lam_ref[0, 0]
    out = jnp.zeros((tq, gw), F32)
    for h in range(GROUP_HEADS):
        base = h * HEAD_DIM
        q1 = jnp.where((lane >= base) & (lane < base + DIFF_QK_DIM), q, 0.0)
        q2 = jnp.where((lane >= base + DIFF_QK_DIM) & (lane < base + HEAD_DIM), q, 0.0)
        lhs_ref[0:tq, :] = q1.astype(BF16)
        lhs_ref[tq:2 * tq, :] = q2.astype(BF16)
        o = _online_softmax(lhs_ref, segments, ATT_KV_CHUNK, m_ref, l_ref, acc_ref)
        oh = o[0:tq] - lam * o[tq:2 * tq]
        out = jnp.where((lane >= base) & (lane < base + HEAD_DIM), oh, out)
    ms = jnp.dot(out * out, seg_ref[...], precision=HIGHEST, preferred_element_type=F32)
    o_ref[...] = out * lax.rsqrt(ms + NORM_EPS) * ng_ref[...] * out_scale


def _attn_call(kind, z, *, n_batch, t_len, ctx_len, ctx_queries, lam=None, norm_g=None, seg=None,
               out_scale=None):
    gw = GROUP_WIDTH
    lat_rows = n_batch * t_len
    if kind == "gqa":
        qw, kw, kcol, vcol = 512, 128, 4, 5
    else:
        qw, kw, kcol, vcol = 256, 256, 1, 2
    ctx_blk0 = lat_rows // ctx_len
    if ctx_queries:
        tq = ctx_len
        grid = (n_batch, 1)
        q_map = lambda b, i: (ctx_blk0 + b, 0)
    else:
        tq = ATT_Q_TILE
        grid = (n_batch, t_len // tq)
        q_map = lambda b, i: (b * (t_len // tq) + i, 0)
    kv_specs = [pl.BlockSpec((ctx_len, kw), lambda b, i: (ctx_blk0 + b, kcol)),
                pl.BlockSpec((ctx_len, kw), lambda b, i: (ctx_blk0 + b, vcol))]
    kv_args = [z, z]
    n_seg = 1
    if not ctx_queries:
        kv_specs += [pl.BlockSpec((t_len, kw), lambda b, i: (b, kcol)),
                     pl.BlockSpec((t_len, kw), lambda b, i: (b, vcol))]
        kv_args += [z, z]
        n_seg = 2
    n_q_rows = n_batch * tq * grid[1]
    out_map = (lambda b, i: (b, 0)) if ctx_queries else q_map
    out_spec = pl.BlockSpec((tq, gw), out_map)
    out_shape = jax.ShapeDtypeStruct((n_q_rows, gw), F32)
    q_spec = pl.BlockSpec((tq, qw), q_map)
    const = lambda b, i: (0, 0)
    scratch = [pltpu.VMEM((2 * tq, kw), BF16), pltpu.VMEM((2 * tq, 1), F32),
               pltpu.VMEM((2 * tq, 1), F32), pltpu.VMEM((2 * tq, kw), F32)]
    if kind == "gqa":
        return pl.pallas_call(
            functools.partial(_gqa_kernel, n_seg=n_seg),
            grid=grid, in_specs=[q_spec] + kv_specs, out_specs=out_spec, out_shape=out_shape,
            scratch_shapes=scratch,
            compiler_params=_cparams("arbitrary", "arbitrary"),
            name="gqa_ctx" if ctx_queries else "gqa_lat",
        )(z, *kv_args)
    return pl.pallas_call(
        functools.partial(_diff_kernel, n_seg=n_seg, out_scale=out_scale),
        grid=grid,
        in_specs=[pl.BlockSpec(memory_space=pltpu.SMEM), q_spec] + kv_specs
        + [pl.BlockSpec((1, gw), const), pl.BlockSpec((gw, gw), const)],
        out_specs=out_spec, out_shape=out_shape,
        scratch_shapes=scratch,
        compiler_params=_cparams("arbitrary", "arbitrary"),
        name="diff_ctx" if ctx_queries else "diff_lat",
    )(lam, z, *kv_args, norm_g, seg)


def _out_proj_kernel(rf_ref, rb_ref, rg_ref, ga_ref, hf_ref, hb_ref, hg_ref, df_ref, x_ref, mod_ref,
                     w_ref, rn_ref, hn_ref, seg_ref, n2_ref, rw_ref, rbias_ref,
                     xo_ref, h2_ref, e_ref, gt_ref):
    d = x_ref.shape[-1]
    gw = GROUP_WIDTH
    seg = seg_ref[...]
    mod = mod_ref[0]

    def readout(o, norm_g, gate):
        ms = jnp.dot(o * o, seg, precision=HIGHEST, preferred_element_type=F32)
        return o * lax.rsqrt(ms + NORM_EPS) * norm_g * (gate * _sigmoid(gate))

    parts = (readout(rf_ref[...] + rb_ref[...], rn_ref[...], rg_ref[...]),
             ga_ref[...],
             readout(hf_ref[...] + hb_ref[...], hn_ref[...], hg_ref[...]),
             df_ref[...])
    acc = jnp.zeros(x_ref.shape, F32)
    for n, part in enumerate(parts):
        acc = acc + jnp.dot(part.astype(BF16), w_ref[n * gw:(n + 1) * gw, :],
                            preferred_element_type=F32)
    x = x_ref[...] + mod[:, 2 * d:3 * d] * acc
    xo_ref[...] = x
    ms = jnp.mean(x * x, axis=-1, keepdims=True)
    h2 = x * lax.rsqrt(ms + NORM_EPS) * n2_ref[...]
    h2 = h2 * (1.0 + mod[:, 4 * d:5 * d]) + mod[:, 3 * d:4 * d]
    h2_ref[...] = h2

    logits = lax.dot_general(rw_ref[...], h2, NT_DIMS, precision=HIGHEST, preferred_element_type=F32)
    score = _sigmoid(logits)
    sel = score + rbias_ref[:, 0:1]
    ng = N_EXPERT_GROUPS
    sc = [score[m * ng:(m + 1) * ng, :] for m in range(EXPERTS_PER_GROUP)]
    sl = [sel[m * ng:(m + 1) * ng, :] for m in range(EXPERTS_PER_GROUP)]
    hi1, lo1 = jnp.maximum(sl[0], sl[1]), jnp.minimum(sl[0], sl[1])
    hi2, lo2 = jnp.maximum(sl[2], sl[3]), jnp.minimum(sl[2], sl[3])
    group_score = jnp.maximum(hi1, hi2) + jnp.maximum(jnp.minimum(hi1, hi2), jnp.maximum(lo1, lo2))
    g_iota = lax.broadcasted_iota(jnp.int32, group_score.shape, 0).astype(F32)
    g_max = jnp.max(group_score, axis=0, keepdims=True)
    best = jnp.min(jnp.where(group_score == g_max, g_iota, float(ng)), axis=0, keepdims=True)
    pick = g_iota == best
    v = [jnp.sum(jnp.where(pick, a, 0.0), axis=0, keepdims=True) for a in sl]
    u = [jnp.sum(jnp.where(pick, a, 0.0), axis=0, keepdims=True) for a in sc]
    rank = []
    for i in range(EXPERTS_PER_GROUP):
        r_i = jnp.zeros(best.shape, F32)
        for j in range(EXPERTS_PER_GROUP):
            if j < i:
                r_i = r_i + jnp.where(v[j] >= v[i], 1.0, 0.0)
            elif j > i:
                r_i = r_i + jnp.where(v[j] > v[i], 1.0, 0.0)
        rank.append(r_i)
    idx, gate = [], []
    for kth in range(2):
        idx.append(sum(jnp.where(rank[i] == kth, float(i), 0.0) for i in range(EXPERTS_PER_GROUP)))
        gate.append(sum(jnp.where(rank[i] == kth, u[i], 0.0) for i in range(EXPERTS_PER_GROUP)))
    tot = gate[0] + gate[1]
    e_ref[0:1, :] = (best * EXPERTS_PER_GROUP + idx[0]).astype(jnp.int32)
    e_ref[1:2, :] = (best * EXPERTS_PER_GROUP + idx[1]).astype(jnp.int32)
    gt_ref[0:1, :] = gate[0] / tot
    gt_ref[1:2, :] = gate[1] / tot


def _out_proj_call(rf, rb, zr, o_gqa, hf, hb, zh, o_df, x_all, mod_all, w_out_bf, rn, hn, seg, n2,
                   rw_t, rbias, n_rows, n_lat_rows, t_len):
    d = x_all.shape[-1]
    tm = ROW_TILE
    gw = GROUP_WIDTH
    n_lat_tiles = n_lat_rows // tm
    tiles_per_seq = t_len // tm
    n_batch = n_lat_rows // t_len

    def mod_idx(i):
        return (jnp.where(i < n_lat_tiles, i // tiles_per_seq, n_batch), 0, 0)

    row = lambda i: (i, 0)
    const = lambda i: (0, 0)
    blk = lambda col: pl.BlockSpec((tm, gw), lambda i: (i, col))
    return pl.pallas_call(
        _out_proj_kernel,
        grid=(n_rows // tm,),
        in_specs=[blk(0), blk(0), blk(3), blk(0), blk(0), blk(0), blk(4), blk(0),
                  pl.BlockSpec((tm, d), row),
                  pl.BlockSpec((1, 1, mod_all.shape[-1]), mod_idx),
                  pl.BlockSpec((4 * gw, d), const),
                  pl.BlockSpec((1, gw), const), pl.BlockSpec((1, gw), const),
                  pl.BlockSpec((gw, gw), const),
                  pl.BlockSpec((1, d), const),
                  pl.BlockSpec((N_EXPERTS, d), const),
                  pl.BlockSpec((N_EXPERTS, 128), const)],
        out_specs=[pl.BlockSpec((tm, d), row), pl.BlockSpec((tm, d), row),
                   pl.BlockSpec((2, tm), lambda i: (0, i)), pl.BlockSpec((2, tm), lambda i: (0, i))],
        out_shape=[jax.ShapeDtypeStruct((n_rows, d), F32), jax.ShapeDtypeStruct((n_rows, d), F32),
                   jax.ShapeDtypeStruct((2, n_rows), jnp.int32),
                   jax.ShapeDtypeStruct((2, n_rows), F32)],
        compiler_params=_cparams("arbitrary"),
        name="out_proj",
    )(rf, rb, zr, o_gqa, hf, hb, zh, o_df, x_all, mod_all, w_out_bf, rn, hn, seg, n2, rw_t, rbias)


def _expert_kernel(be_ref, nb_ref, dst_hbm, gate_ref, h2_hbm, wg_ref, wu_ref, wd_ref, ys_hbm,
                   idx_ref, x_ref, y_ref, sem_i, sem_g, sem_s, *, n_tok):
    i = pl.program_id(0)
    rows = x_ref.shape[0]

    @pl.when(i < nb_ref[0])
    def _():
        cp = pltpu.make_async_copy(dst_hbm.at[i], idx_ref, sem_i)
        cp.start()
        cp.wait()

        def gather(r, c):
            dst = idx_ref[r]
            tok = jnp.where(dst < 0, 0, jnp.where(dst >= n_tok, dst - n_tok, dst))
            pltpu.make_async_copy(h2_hbm.at[pl.ds(tok, 1)], x_ref.at[pl.ds(r, 1)], sem_g).start()
            return c
        lax.fori_loop(0, rows, gather, 0)
        pltpu.make_async_copy(x_ref, x_ref, sem_g).wait()

        xb = x_ref[...].astype(BF16)
        a = jnp.dot(xb, wg_ref[0], preferred_element_type=F32)
        u = jnp.dot(xb, wu_ref[0], preferred_element_type=F32)
        hmid = (a * _sigmoid(a) * u).astype(BF16)
        y_ref[...] = jnp.dot(hmid, wd_ref[0], preferred_element_type=F32) * gate_ref[:, 0:1]

        def scatter(r, cnt):
            dst = idx_ref[r]

            @pl.when(dst >= 0)
            def _():
                pltpu.make_async_copy(y_ref.at[pl.ds(r, 1)], ys_hbm.at[pl.ds(dst, 1)], sem_s).start()
            return cnt + (dst >= 0).astype(jnp.int32)
        n_valid = lax.fori_loop(0, rows, scatter, 0)

        def drain(r, c):
            pltpu.make_async_copy(y_ref.at[pl.ds(0, 1)], ys_hbm.at[pl.ds(0, 1)], sem_s).wait()
            return c
        lax.fori_loop(0, n_valid, drain, 0)


def _expert_call(block_expert, n_used, row_dst, row_gate, h2, wg, wu, wd, n_tok):
    n_blocks = block_expert.shape[0]
    d = h2.shape[-1]
    ff = wg.shape[-1]
    rows = MOE_BLOCK
    grid_spec = pltpu.PrefetchScalarGridSpec(
        num_scalar_prefetch=2,
        grid=(n_blocks,),
        in_specs=[pl.BlockSpec(memory_space=pl.ANY),
                  pl.BlockSpec((rows, 1), lambda i, be, nb: (i, 0)),
                  pl.BlockSpec(memory_space=pl.ANY),
                  pl.BlockSpec((1, d, ff), lambda i, be, nb: (be[i], 0, 0)),
                  pl.BlockSpec((1, d, ff), lambda i, be, nb: (be[i], 0, 0)),
                  pl.BlockSpec((1, ff, d), lambda i, be, nb: (be[i], 0, 0))],
        out_specs=pl.BlockSpec(memory_space=pl.ANY),
        scratch_shapes=[pltpu.SMEM((rows,), jnp.int32),
                        pltpu.VMEM((rows, d), F32),
                        pltpu.VMEM((rows, d), F32),
                        pltpu.SemaphoreType.DMA, pltpu.SemaphoreType.DMA, pltpu.SemaphoreType.DMA],
    )
    return pl.pallas_call(
        functools.partial(_expert_kernel, n_tok=n_tok),
        grid_spec=grid_spec,
        out_shape=jax.ShapeDtypeStruct((2 * n_tok, d), F32),
        compiler_params=_cparams("arbitrary"),
        name="experts",
    )(block_expert, n_used, row_dst, row_gate, h2, wg, wu, wd)


def _dispatch_plan(e_t, g_t, n_tok):
    m = 2 * n_tok
    e_flat = e_t.reshape(m)
    order = jnp.argsort(e_flat)
    counts = jnp.bincount(e_flat, length=N_EXPERTS)
    padded = (counts + MOE_BLOCK - 1) // MOE_BLOCK * MOE_BLOCK
    start = jnp.cumsum(counts) - counts
    pend = jnp.cumsum(padded)
    pstart = pend - padded
    e_s = e_flat[order]
    dest = (pstart[e_s] + jnp.arange(m) - start[e_s]).astype(jnp.int32)
    n_blocks = -(-m // MOE_BLOCK) + N_EXPERTS
    n_rows = n_blocks * MOE_BLOCK
    row_dst = jnp.full((n_rows,), -1, jnp.int32).at[dest].set(order.astype(jnp.int32))
    row_gate = jnp.zeros((n_rows,), F32).at[dest].set(g_t.reshape(m)[order])
    block_expert = jnp.clip(jnp.searchsorted(pend, jnp.arange(n_blocks) * MOE_BLOCK, side='right'),
                            0, N_EXPERTS - 1).astype(jnp.int32)
    n_used = (pend[-1] // MOE_BLOCK).astype(jnp.int32).reshape(1)
    return block_expert, n_used, row_dst.reshape(n_blocks, MOE_BLOCK), row_gate.reshape(n_rows, 1)


def _combine_kernel(x_ref, y0_ref, y1_ref, mod_ref, fg_ref, o_ref, *, final_norm):
    d = x_ref.shape[-1]
    mod = mod_ref[0]
    x = x_ref[...] + mod[:, 5 * d:6 * d] * (y0_ref[...] + y1_ref[...])
    if final_norm:
        ms = jnp.mean(x * x, axis=-1, keepdims=True)
        x = x * lax.rsqrt(ms + NORM_EPS) * fg_ref[...]
    o_ref[...] = x


def _combine_call(x_new, ys, mod_all, final_g, n_rows, n_lat_rows, t_len, final_norm):
    d = x_new.shape[-1]
    tm = ROW_TILE
    n_lat_tiles = n_lat_rows // tm
    tiles_per_seq = t_len // tm
    n_batch = n_lat_rows // t_len
    n_tiles = n_rows // tm

    def mod_idx(i):
        return (jnp.where(i < n_lat_tiles, i // tiles_per_seq, n_batch), 0, 0)

    return pl.pallas_call(
        functools.partial(_combine_kernel, final_norm=final_norm),
        grid=(n_tiles,),
        in_specs=[pl.BlockSpec((tm, d), lambda i: (i, 0)),
                  pl.BlockSpec((tm, d), lambda i: (i, 0)),
                  pl.BlockSpec((tm, d), lambda i: (i + n_tiles, 0)),
                  pl.BlockSpec((1, 1, mod_all.shape[-1]), mod_idx),
                  pl.BlockSpec((1, d), lambda i: (0, 0))],
        out_specs=pl.BlockSpec((tm, d), lambda i: (i, 0)),
        out_shape=jax.ShapeDtypeStruct((n_rows, d), F32),
        compiler_params=_cparams("arbitrary"),
        name="combine",
    )(x_new, ys, ys, mod_all, final_g)


def _rope_tables(t_len, rot_dim, pad_rows):
    n_freq = rot_dim // 4
    half = rot_dim // 2
    inv_freq = ROPE_BASE ** (-jnp.arange(n_freq, dtype=F32) / n_freq)
    pos = jnp.arange(t_len)
    row = (pos // GRID_W).astype(F32)
    col = (pos % GRID_W).astype(F32)
    ang = jnp.concatenate([row[:, None] * inv_freq, col[:, None] * inv_freq], axis=-1)
    cos, sin = jnp.cos(ang), jnp.sin(ang)
    reps = GROUP_WIDTH // rot_dim
    zeros = jnp.zeros_like(sin)
    cos_f = jnp.tile(jnp.concatenate([cos, cos], -1), (1, reps))
    sin_a = jnp.tile(jnp.concatenate([-sin, zeros], -1), (1, reps))
    sin_b = jnp.tile(jnp.concatenate([zeros, sin], -1), (1, reps))
    pad1 = jnp.ones((pad_rows, GROUP_WIDTH), F32)
    pad0 = jnp.zeros((pad_rows, GROUP_WIDTH), F32)
    return (jnp.concatenate([cos_f, pad1], 0), jnp.concatenate([sin_a, pad0], 0),
            jnp.concatenate([sin_b, pad0], 0))


def kernel(x, c, ctx, c_ctx, w_mod, b_mod, norm1_g, norm2_g, w_in, ret_decay_logit, ret_norm_g,
           gqa_qnorm_g, gqa_knorm_g, hgrn_lb_logit, hgrn_norm_g, diff_lambda, diff_norm_g, w_out,
           router_w, router_bias, moe_w_gate, moe_w_up, moe_w_down, final_norm_g):
    n_batch, t_len, d = x.shape
    ctx_len = ctx.shape[1]
    depth = w_mod.shape[0]
    gw = GROUP_WIDTH
    lat_rows = n_batch * t_len
    all_rows = lat_rows + n_batch * ctx_len
    assert t_len % ROW_TILE == 0 and ctx_len % ROW_TILE == 0 and t_len % ATT_KV_CHUNK == 0
    assert ctx_len % REC_CHUNK == 0 and ctx_len <= ATT_KV_CHUNK and n_batch + 1 <= 8

    x_all = jnp.concatenate([x.reshape(lat_rows, d), ctx.reshape(n_batch * ctx_len, d)], axis=0)
    cvec = jnp.zeros((8, d), F32).at[:n_batch].set(c).at[n_batch].set(c_ctx)

    tabs = _rope_tables(t_len, HEAD_DIM, ROW_TILE) + _rope_tables(t_len, DIFF_QK_DIM, ROW_TILE)
    head_of = jnp.arange(gw) // HEAD_DIM
    same_head = head_of[:, None] == head_of[None, :]
    seg = same_head.astype(F32) / HEAD_DIM
    bd = same_head.astype(BF16)

    sm = jax.nn.softmax(hgrn_lb_logit.astype(F32), axis=1)
    lower_bounds = jnp.cumsum(sm, axis=1) - sm[:, :1]
    log_gamma = jax.nn.log_sigmoid(ret_decay_logit.astype(F32))

    perm = (jnp.arange(N_EXPERTS) % N_EXPERT_GROUPS) * EXPERTS_PER_GROUP + jnp.arange(N_EXPERTS) // N_EXPERT_GROUPS
    rw_t = router_w.T[perm]
    rbias = jnp.broadcast_to(router_bias.astype(F32)[perm][:, None], (N_EXPERTS, 128))

    tile = lambda g: jnp.tile(g.astype(F32), GROUP_HEADS).reshape(1, gw)

    out = None
    for layer in range(depth):
        need_ctx = layer < depth - 1
        lambda_init = 0.8 - 0.6 * math.exp(-0.3 * layer)
        mod_all = _mod_call(cvec, w_mod[layer], b_mod[layer]).reshape(8, 1, 6 * d)

        w_l = w_in[layer]
        w_bf = w_l.astype(BF16)
        wvt_bf = jnp.concatenate([w_l[:, 2 * gw:3 * gw], w_l[:, 9 * gw:10 * gw]], axis=1).T.astype(BF16)
        zr, zg, zh, zd, vt_all = _in_proj_call(
            x_all, mod_all, norm1_g[layer].reshape(1, d), w_bf, wvt_bf, tabs,
            tile(gqa_qnorm_g[layer]), tile(gqa_knorm_g[layer])[:, :128], seg, lat_rows, t_len)

        rec_kw = dict(n_batch=n_batch, t_len=t_len, ctx_len=ctx_len)
        ret_par = jnp.repeat(log_gamma[layer], HEAD_DIM, axis=-1)
        rf, rb = _rec_call(zr, vt_all, ret_par, bd, hgrn=False, cols=(0, 1, 1, 2), vt_row=0, **rec_kw)
        hf, hb = _rec_call(zh, vt_all, lower_bounds[:, layer], bd, hgrn=True, cols=(0, 1, 2, 3),
                           vt_row=1, **rec_kw)

        lp = diff_lambda[layer].astype(F32)
        lam = (jnp.exp(jnp.sum(lp[0] * lp[1])) - jnp.exp(jnp.sum(lp[2] * lp[3])) + lambda_init)
        lam = lam.reshape(1, 1)
        att_kw = dict(n_batch=n_batch, t_len=t_len, ctx_len=ctx_len)
        diff_kw = dict(lam=lam, norm_g=tile(diff_norm_g[layer]), seg=seg, out_scale=1.0 - lambda_init)
        o_gqa = _attn_call("gqa", zg, ctx_queries=False, **att_kw)
        o_df = _attn_call("diff", zd, ctx_queries=False, **att_kw, **diff_kw)
        n_rows = lat_rows
        if need_ctx:
            o_gqa = jnp.concatenate([o_gqa, _attn_call("gqa", zg, ctx_queries=True, **att_kw)], 0)
            o_df = jnp.concatenate([o_df, _attn_call("diff", zd, ctx_queries=True, **att_kw, **diff_kw)], 0)
            n_rows = all_rows

        x_new, h2, e_t, g_t = _out_proj_call(
            rf, rb, zr, o_gqa, hf, hb, zh, o_df, x_all, mod_all, w_out[layer].astype(BF16),
            tile(ret_norm_g[layer]), tile(hgrn_norm_g[layer]), seg, norm2_g[layer].reshape(1, d),
            rw_t, rbias, n_rows, lat_rows, t_len)

        plan = _dispatch_plan(e_t, g_t, n_rows)
        ys = _expert_call(*plan, h2, moe_w_gate[layer].astype(BF16), moe_w_up[layer].astype(BF16),
                          moe_w_down[layer].astype(BF16), n_rows)
        last = layer == depth - 1
        x_next = _combine_call(x_new, ys, mod_all, final_norm_g.reshape(1, d), n_rows, lat_rows, t_len,
                               final_norm=last)
        if last:
            out = x_next[:lat_rows].reshape(n_batch, t_len, d)
        else:
            x_all = x_next
    return out
```

```python
import functools
import math

import jax
import jax.numpy as jnp
from jax import lax
from jax.experimental import pallas as pl
from jax.experimental.pallas import tpu as pltpu

F32 = jnp.float32
BF16 = jnp.bfloat16
HIGHEST = lax.Precision.HIGHEST

HEAD_DIM = 64
GROUP_HEADS = 4
GROUP_WIDTH = GROUP_HEADS * HEAD_DIM
GQA_KV_HEADS = 2
DIFF_QK_DIM = HEAD_DIM // 2
GRID_W = 64
ROPE_BASE = 10000.0
N_EXPERTS = 32
N_EXPERT_GROUPS = 8
EXPERTS_PER_GROUP = 4
MOE_BLOCK = 256
NORM_EPS = 1e-6

ROW_TILE = 256
REC_CHUNK = 128
REC_SUB = 16
ATT_Q_TILE = 256
ATT_KV_CHUNK_MAX = 1536
VMEM_LIMIT = 56 * 1024 * 1024

NT_DIMS = (((1,), (1,)), ((), ()))
LOG2E = math.log2(math.e)


def _sigmoid(v):
    return 1.0 / (1.0 + jnp.exp(-v))


def _cparams(*sem):
    return pltpu.CompilerParams(dimension_semantics=sem, vmem_limit_bytes=VMEM_LIMIT)


def _mod_kernel(c_ref, w_ref, b_ref, o_ref):
    cv = c_ref[...]
    s = cv * _sigmoid(cv)
    o_ref[...] = jnp.dot(s, w_ref[...], precision=HIGHEST, preferred_element_type=F32) + b_ref[...]


def _mod_call(cvec, w_mod, b_mod):
    d, n = w_mod.shape
    tn = 1024
    return pl.pallas_call(
        _mod_kernel,
        grid=(n // tn,),
        in_specs=[pl.BlockSpec((8, d), lambda j: (0, 0)),
                  pl.BlockSpec((d, tn), lambda j: (0, j)),
                  pl.BlockSpec((1, tn), lambda j: (0, j))],
        out_specs=pl.BlockSpec((8, tn), lambda j: (0, j)),
        out_shape=jax.ShapeDtypeStruct((8, n), F32),
        compiler_params=_cparams("arbitrary"),
        name="mod",
    )(cvec, w_mod, b_mod.reshape(1, n))


def _rope(v, cos, sin_a, sin_b, half):
    n = v.shape[-1]
    return v * cos + pltpu.roll(v, n - half, 1) * sin_a + pltpu.roll(v, half, 1) * sin_b


def _in_proj_kernel(x_ref, mod_ref, g1_ref, w_ref, wvt_ref, c64_ref, sa64_ref, sb64_ref,
                    c32_ref, sa32_ref, sb32_ref, qg_ref, kg_ref, seg_ref,
                    zr_ref, zg_ref, zh_ref, zd_ref, vt_ref, kvg_ref, kvd_ref):
    d = x_ref.shape[-1]
    gw = GROUP_WIDTH
    x = x_ref[...]
    ms = jnp.mean(x * x, axis=-1, keepdims=True)
    h = x * lax.rsqrt(ms + NORM_EPS) * g1_ref[...]
    mod = mod_ref[0]
    h = h * (1.0 + mod[:, d:2 * d]) + mod[:, 0:d]
    hb = h.astype(BF16)

    def proj(c0, c1):
        return jnp.dot(hb, w_ref[:, c0:c1], preferred_element_type=F32)

    c64, sa64, sb64 = c64_ref[...], sa64_ref[...], sb64_ref[...]
    c32, sa32, sb32 = c32_ref[...], sa32_ref[...], sb32_ref[...]
    seg = seg_ref[...]

    zr_ref[:, 0:gw] = _rope(proj(0, gw) * HEAD_DIM ** -0.5, c64, sa64, sb64, 32)
    zr_ref[:, gw:2 * gw] = _rope(proj(gw, 2 * gw), c64, sa64, sb64, 32)
    zr_ref[:, 2 * gw:4 * gw] = proj(2 * gw, 4 * gw)

    o = 4 * gw
    q = proj(o, o + gw)
    qms = jnp.dot(q * q, seg, precision=HIGHEST, preferred_element_type=F32)
    q = q * lax.rsqrt(qms + NORM_EPS) * qg_ref[...]
    q = _rope(q, c64, sa64, sb64, 32) * (HEAD_DIM ** -0.5 * LOG2E)
    lane = lax.broadcasted_iota(jnp.int32, (q.shape[0], 128), 1)
    lo = lane < HEAD_DIM
    qa, qb = q[:, 0:128], q[:, 128:256]
    zg_ref[:, 0:128] = jnp.where(lo, qa, 0.0).astype(BF16)
    zg_ref[:, 128:256] = jnp.where(lo, pltpu.roll(qa, 64, 1), 0.0).astype(BF16)
    zg_ref[:, 256:384] = jnp.where(lo, 0.0, pltpu.roll(qb, 64, 1)).astype(BF16)
    zg_ref[:, 384:512] = jnp.where(lo, 0.0, qb).astype(BF16)
    k = proj(o + gw, o + gw + 128)
    kms = jnp.dot(k * k, seg[0:128, 0:128], precision=HIGHEST, preferred_element_type=F32)
    k = k * lax.rsqrt(kms + NORM_EPS) * kg_ref[...]
    kvg_ref[:, 0:128] = _rope(k, c64[:, 0:128], sa64[:, 0:128], sb64[:, 0:128], 32).astype(BF16)
    kvg_ref[:, 128:256] = proj(o + gw + 128, o + gw + 256).astype(BF16)

    o = 6 * gw
    zh_ref[...] = proj(o, o + 5 * gw)

    o = 11 * gw
    zd_ref[...] = (_rope(proj(o, o + gw), c32, sa32, sb32, 16) * (DIFF_QK_DIM ** -0.5 * LOG2E)).astype(BF16)
    kvd_ref[:, 0:gw] = _rope(proj(o + gw, o + 2 * gw), c32, sa32, sb32, 16).astype(BF16)
    kvd_ref[:, gw:2 * gw] = proj(o + 2 * gw, o + 3 * gw).astype(BF16)

    vt_ref[...] = lax.dot_general(wvt_ref[...], hb, NT_DIMS, preferred_element_type=F32).astype(BF16)


def _in_proj_call(x_all, mod_all, g1, w_bf, wvt_bf, tabs, qg, kg, seg, n_lat_rows, t_len):
    r, d = x_all.shape
    tm = ROW_TILE
    gw = GROUP_WIDTH
    n_lat_tiles = n_lat_rows // tm
    tiles_per_seq = t_len // tm
    n_batch = n_lat_rows // t_len

    def mod_idx(i):
        return (jnp.where(i < n_lat_tiles, i // tiles_per_seq, n_batch), 0, 0)

    def tab_idx(i):
        return (jnp.where(i < n_lat_tiles, i % tiles_per_seq, tiles_per_seq), 0)

    ctx_tiles = (r - n_lat_rows) // n_batch // tm
    seq_tiles = tiles_per_seq + ctx_tiles

    def kv_idx(i):
        ci = i - n_lat_tiles
        lat = (i // tiles_per_seq) * seq_tiles + ctx_tiles + i % tiles_per_seq
        ctx = (ci // ctx_tiles) * seq_tiles + ci % ctx_tiles
        return (jnp.where(i < n_lat_tiles, lat, ctx), 0)

    row = lambda i: (i, 0)
    const = lambda i: (0, 0)
    tab_spec = pl.BlockSpec((tm, gw), tab_idx)
    return pl.pallas_call(
        _in_proj_kernel,
        grid=(r // tm,),
        in_specs=[pl.BlockSpec((tm, d), row),
                  pl.BlockSpec((1, 1, mod_all.shape[-1]), mod_idx),
                  pl.BlockSpec((1, d), const),
                  pl.BlockSpec(w_bf.shape, const),
                  pl.BlockSpec(wvt_bf.shape, const),
                  tab_spec, tab_spec, tab_spec, tab_spec, tab_spec, tab_spec,
                  pl.BlockSpec((1, gw), const),
                  pl.BlockSpec((1, 128), const),
                  pl.BlockSpec((gw, gw), const)],
        out_specs=[pl.BlockSpec((tm, 4 * gw), row),
                   pl.BlockSpec((tm, 2 * gw), row),
                   pl.BlockSpec((tm, 5 * gw), row),
                   pl.BlockSpec((tm, gw), row),
                   pl.BlockSpec((2 * gw, tm), lambda i: (0, i)),
                   pl.BlockSpec((tm, gw), kv_idx),
                   pl.BlockSpec((tm, 2 * gw), kv_idx)],
        out_shape=[jax.ShapeDtypeStruct((r, 4 * gw), F32),
                   jax.ShapeDtypeStruct((r, 2 * gw), BF16),
                   jax.ShapeDtypeStruct((r, 5 * gw), F32),
                   jax.ShapeDtypeStruct((r, gw), BF16),
                   jax.ShapeDtypeStruct((2 * gw, r), BF16),
                   jax.ShapeDtypeStruct((r, gw), BF16),
                   jax.ShapeDtypeStruct((r, 2 * gw), BF16)],
        compiler_params=_cparams("arbitrary"),
        name="in_proj",
    )(x_all, mod_all, g1, w_bf, wvt_bf, *tabs, qg, kg, seg)


def _rec_direction(q_ref, k_ref, v_ref, vt_ref, par, bd_ref, o_ref, s_ref,
                   lc_ref, qt_ref, kh_ref, kk_ref, tot_ref, a_ref, *, hgrn, reverse):
    c, gw = q_ref.shape
    sub = REC_SUB
    n_sub = c // sub
    kin = k_ref[...]
    if hgrn:
        sig = _sigmoid(kin)
        logf = jnp.log(par + (1.0 - par) * sig)
        kk = (1.0 - par) * (1.0 - sig)
    else:
        logf = jnp.broadcast_to(par, (c, gw))
        kk = kin
    r_i = lax.broadcasted_iota(jnp.int32, (c, c), 0)
    c_i = lax.broadcasted_iota(jnp.int32, (c, c), 1)
    shift = sub.bit_length() - 1
    same = jnp.right_shift(r_i, shift) == jnp.right_shift(c_i, shift)
    if reverse:
        incl = same & (c_i >= r_i)
        excl = same & (c_i < r_i)
    else:
        incl = same & (c_i <= r_i)
        excl = same & (c_i > r_i)
    lc = jnp.dot(jnp.where(incl, 1.0, 0.0), logf, precision=HIGHEST, preferred_element_type=F32)
    rr = jnp.dot(jnp.where(excl, 1.0, 0.0), logf, precision=HIGHEST, preferred_element_type=F32)
    lc_ref[...] = lc
    qt_ref[...] = (q_ref[...] * jnp.exp(lc)).astype(BF16)
    kh_ref[...] = kk * jnp.exp(rr)
    kk_ref[...] = kk
    tot_ref[...] = lc + rr
    vt = vt_ref[...]
    bd = bd_ref[...]
    t_loc = lax.broadcasted_iota(jnp.int32, (sub, gw), 0)
    row_id = lax.broadcasted_iota(jnp.int32, (c, gw), 0)
    head_shift = HEAD_DIM.bit_length() - 1
    same_head = (jnp.right_shift(lax.broadcasted_iota(jnp.int32, (gw, gw), 0), head_shift)
                 == jnp.right_shift(lax.broadcasted_iota(jnp.int32, (gw, gw), 1), head_shift))

    def body(j, carry):
        a = (n_sub - 1 - j) if reverse else j
        base = pl.multiple_of(a * sub, sub)
        rows = pl.ds(base, sub)
        s_t = s_ref[...]
        o_inter = lax.dot_general(qt_ref[rows, :], s_t.astype(BF16), NT_DIMS,
                                  preferred_element_type=F32)
        lc_a = lc_ref[rows, :]
        q_a = q_ref[rows, :]
        k_a = kk_ref[rows, :]
        v_a = v_ref[rows, :]
        for s in range(sub):
            valid = (t_loc <= s) if reverse else (t_loc >= s)
            e = jnp.where(valid, jnp.exp(lc_a - lc_a[s:s + 1, :]), 0.0)
            a_ref[s * sub:(s + 1) * sub, :] = (q_a * e * k_a[s:s + 1, :]).astype(BF16)
        b = jnp.dot(a_ref[...], bd, preferred_element_type=F32)
        o_intra = jnp.zeros((sub, gw), F32)
        for s in range(sub):
            o_intra = o_intra + b[s * sub:(s + 1) * sub, :] * v_a[s:s + 1, :]
        o_ref[rows, :] = o_inter + o_intra
        in_sub = (row_id >= base) & (row_id < base + sub)
        kh_m = jnp.where(in_sub, kh_ref[...], 0.0).astype(BF16)
        u = jnp.dot(vt, kh_m, preferred_element_type=F32)
        decay = jnp.exp(tot_ref[pl.ds(base, 1), :])
        s_ref[...] = s_t * decay + jnp.where(same_head, u, 0.0)
        return carry

    lax.fori_loop(0, n_sub, body, 0)


def _rec_kernel(qf_ref, kf_ref, vf_ref, vtf_ref, qb_ref, kb_ref, vb_ref, vtb_ref, par_ref, bd_ref,
                of_ref, ob_ref, sf_ref, sb_ref, lc_ref, qt_ref, kh_ref, kk_ref, tot_ref, a_ref,
                *, hgrn):
    @pl.when(pl.program_id(1) == 0)
    def _():
        sf_ref[...] = jnp.zeros_like(sf_ref)
        sb_ref[...] = jnp.zeros_like(sb_ref)

    tmp = (lc_ref, qt_ref, kh_ref, kk_ref, tot_ref, a_ref)
    _rec_direction(qf_ref, kf_ref, vf_ref, vtf_ref, par_ref[0:1, :], bd_ref, of_ref, sf_ref, *tmp,
                   hgrn=hgrn, reverse=False)
    _rec_direction(qb_ref, kb_ref, vb_ref, vtb_ref, par_ref[1:2, :], bd_ref, ob_ref, sb_ref, *tmp,
                   hgrn=hgrn, reverse=True)


def _rec_call(z, vt_all, par, bd, *, hgrn, n_batch, t_len, ctx_len, cols, vt_row):
    r = z.shape[0]
    c = REC_CHUNK
    gw = GROUP_WIDTH
    n_ctx = ctx_len // c
    n_lat = t_len // c
    lat_blocks = n_batch * n_lat

    def fwd_blk(b, i):
        return jnp.where(i < n_ctx, lat_blocks + b * n_ctx + i, b * n_lat + i - n_ctx)

    def bwd_blk(b, i):
        return jnp.where(i < n_ctx, lat_blocks + b * n_ctx + (n_ctx - 1 - i),
                         b * n_lat + (n_lat - 1 - (i - n_ctx)))

    def zspec(blk, col):
        return pl.BlockSpec((c, gw), lambda b, i: (blk(b, i), col))

    def vtspec(blk):
        return pl.BlockSpec((gw, c), lambda b, i: (vt_row, blk(b, i)))

    const = lambda b, i: (0, 0)
    cq, ckf, ckb, cv = cols
    out_f, out_b = pl.pallas_call(
        functools.partial(_rec_kernel, hgrn=hgrn),
        grid=(n_batch, n_ctx + n_lat),
        in_specs=[zspec(fwd_blk, cq), zspec(fwd_blk, ckf), zspec(fwd_blk, cv), vtspec(fwd_blk),
                  zspec(bwd_blk, cq), zspec(bwd_blk, ckb), zspec(bwd_blk, cv), vtspec(bwd_blk),
                  pl.BlockSpec((2, gw), const),
                  pl.BlockSpec((gw, gw), const)],
        out_specs=[pl.BlockSpec((c, gw), lambda b, i: (fwd_blk(b, i), 0)),
                   pl.BlockSpec((c, gw), lambda b, i: (bwd_blk(b, i), 0))],
        out_shape=[jax.ShapeDtypeStruct((r, gw), F32), jax.ShapeDtypeStruct((r, gw), F32)],
        scratch_shapes=[pltpu.VMEM((gw, gw), F32), pltpu.VMEM((gw, gw), F32),
                        pltpu.VMEM((c, gw), F32), pltpu.VMEM((c, gw), BF16),
                        pltpu.VMEM((c, gw), F32), pltpu.VMEM((c, gw), F32),
                        pltpu.VMEM((c, gw), F32), pltpu.VMEM((REC_SUB * REC_SUB, gw), BF16)],
        compiler_params=_cparams("arbitrary", "arbitrary"),
        name="hgrn_rec" if hgrn else "ret_rec",
    )(z, z, z, vt_all, z, z, z, vt_all, par, bd)
    return out_f, out_b


def _online_softmax(lhs_ref, k_ref, v_ref, kv_chunk, scr):
    m_ref, l_ref, acc_ref = scr[0:3]
    s_refs, p_refs, a_refs = scr[3:5], scr[5:7], scr[7:9]
    n_heads = lhs_ref.shape[0]
    n = k_ref.shape[0] // kv_chunk
    steps = n_heads * n
    assert steps % 2 == 0 and steps >= 4
    m_ref[...] = jnp.full(m_ref.shape, -jnp.inf, F32)
    l_ref[...] = jnp.zeros(l_ref.shape, F32)
    acc_ref[...] = jnp.zeros(acc_ref.shape, F32)

    def split(t):
        if isinstance(t, int):
            return t // n, pl.ds((t % n) * kv_chunk, kv_chunk)
        h = t // n
        return h, pl.ds(pl.multiple_of((t - h * n) * kv_chunk, kv_chunk), kv_chunk)

    def qk(t, slot):
        h, rows = split(t)
        s_refs[slot][...] = lax.dot_general(lhs_ref[h], k_ref[rows, :], NT_DIMS,
                                            preferred_element_type=F32)

    def sm(t, slot):
        h, _ = split(t)
        s = s_refs[slot][...]
        m = m_ref[h]
        m_new = jnp.maximum(m, jnp.max(s, axis=-1, keepdims=True))
        alpha = jnp.exp2(m - m_new)
        p = jnp.exp2(s - m_new)
        m_ref[h] = m_new
        l_ref[h] = alpha * l_ref[h] + jnp.sum(p, axis=-1, keepdims=True)
        a_refs[slot][...] = alpha
        p_refs[slot][...] = p.astype(BF16)

    def pv(t, slot):
        h, rows = split(t)
        acc_ref[h] = a_refs[slot][...] * acc_ref[h] + jnp.dot(
            p_refs[slot][...], v_ref[rows, :], preferred_element_type=F32)

    qk(0, 0)
    qk(1, 1)
    sm(0, 0)

    def body(i, carry):
        t = 2 * i + 1
        qk(t + 1, 0)
        sm(t, 1)
        pv(t - 1, 0)
        qk(t + 2, 1)
        sm(t + 1, 0)
        pv(t, 1)
        return carry

    lax.fori_loop(0, (steps - 2) // 2, body, 0)
    sm(steps - 1, 1)
    pv(steps - 2, 0)
    pv(steps - 1, 1)


def _gqa_kernel(q_ref, k_ref, v_ref, o_ref, lhs_ref, *scr, kv_chunk):
    tq = q_ref.shape[0]
    lane = lax.broadcasted_iota(jnp.int32, (tq, 128), 1)
    lo = lane < HEAD_DIM
    for j in range(GQA_KV_HEADS):
        lhs_ref[j, 0:tq, :] = q_ref[:, (2 * j) * 128:(2 * j + 1) * 128]
        lhs_ref[j, tq:2 * tq, :] = q_ref[:, (2 * j + 1) * 128:(2 * j + 2) * 128]
    _online_softmax(lhs_ref, k_ref, v_ref, kv_chunk, scr)
    l_ref, acc_ref = scr[1], scr[2]
    o0 = acc_ref[0] / l_ref[0]
    o1 = acc_ref[1] / l_ref[1]
    o_ref[:, 0:128] = jnp.where(lo, o0[0:tq], pltpu.roll(o0[tq:2 * tq], 64, 1))
    o_ref[:, 128:256] = jnp.where(lo, pltpu.roll(o1[0:tq], 64, 1), o1[tq:2 * tq])


def _diff_kernel(lam_ref, q_ref, k_ref, v_ref, ng_ref, seg_ref, o_ref, lhs_ref, *scr, kv_chunk,
                 out_scale):
    tq, gw = q_ref.shape
    lane = lax.broadcasted_iota(jnp.int32, (tq, gw), 1)
    q = q_ref[...].astype(F32)
    lam = lam_ref[0, 0]
    for h in range(GROUP_HEADS):
        base = h * HEAD_DIM
        q1 = jnp.where((lane >= base) & (lane < base + DIFF_QK_DIM), q, 0.0)
        q2 = jnp.where((lane >= base + DIFF_QK_DIM) & (lane < base + HEAD_DIM), q, 0.0)
        lhs_ref[h, 0:tq, :] = q1.astype(BF16)
        lhs_ref[h, tq:2 * tq, :] = q2.astype(BF16)
    _online_softmax(lhs_ref, k_ref, v_ref, kv_chunk, scr)
    l_ref, acc_ref = scr[1], scr[2]
    out = jnp.zeros((tq, gw), F32)
    for h in range(GROUP_HEADS):
        base = h * HEAD_DIM
        o = acc_ref[h] / l_ref[h]
        oh = o[0:tq] - lam * o[tq:2 * tq]
        out = jnp.where((lane >= base) & (lane < base + HEAD_DIM), oh, out)
    ms = jnp.dot(out * out, seg_ref[...], precision=HIGHEST, preferred_element_type=F32)
    o_ref[...] = out * lax.rsqrt(ms + NORM_EPS) * ng_ref[...] * out_scale


def _kv_chunk(s_len, n_heads):
    best = None
    for kc in range(128, min(s_len, ATT_KV_CHUNK_MAX) + 1, 128):
        if s_len % kc == 0 and n_heads * (s_len // kc) >= 4:
            best = kc
    assert best is not None, s_len
    return best


def _attn_call(kind, zq, kv, *, n_batch, t_len, ctx_len, ctx_queries, lam=None, norm_g=None,
               seg=None, out_scale=None):
    gw = GROUP_WIDTH
    lat_rows = n_batch * t_len
    s_all = ctx_len + t_len
    if kind == "gqa":
        qw, kw, n_heads = 512, 128, GQA_KV_HEADS
    else:
        qw, kw, n_heads = 256, 256, GROUP_HEADS
    if ctx_queries:
        tq, s_len = ctx_len, ctx_len
        grid = (n_batch, 1)
        q_map = lambda b, i: (lat_rows // ctx_len + b, 0)
        out_map = lambda b, i: (b, 0)
        kv_blk = lambda b: b * (s_all // ctx_len)
    else:
        tq, s_len = ATT_Q_TILE, s_all
        grid = (n_batch, t_len // tq)
        q_map = lambda b, i: (b * (t_len // tq) + i, 0)
        out_map = q_map
        kv_blk = lambda b: b
    kc = _kv_chunk(s_len, n_heads)
    kv_specs = [pl.BlockSpec((s_len, kw), lambda b, i: (kv_blk(b), 0)),
                pl.BlockSpec((s_len, kw), lambda b, i: (kv_blk(b), 1))]
    out_spec = pl.BlockSpec((tq, gw), out_map)
    out_shape = jax.ShapeDtypeStruct((n_batch * tq * grid[1], gw), F32)
    q_spec = pl.BlockSpec((tq, qw), q_map)
    const = lambda b, i: (0, 0)
    m_rows = 2 * tq
    scratch = [pltpu.VMEM((n_heads, m_rows, kw), BF16),
               pltpu.VMEM((n_heads, m_rows, 1), F32),
               pltpu.VMEM((n_heads, m_rows, 1), F32),
               pltpu.VMEM((n_heads, m_rows, kw), F32),
               pltpu.VMEM((m_rows, kc), F32), pltpu.VMEM((m_rows, kc), F32),
               pltpu.VMEM((m_rows, kc), BF16), pltpu.VMEM((m_rows, kc), BF16),
               pltpu.VMEM((m_rows, 1), F32), pltpu.VMEM((m_rows, 1), F32)]
    if kind == "gqa":
        return pl.pallas_call(
            functools.partial(_gqa_kernel, kv_chunk=kc),
            grid=grid, in_specs=[q_spec] + kv_specs, out_specs=out_spec, out_shape=out_shape,
            scratch_shapes=scratch,
            compiler_params=_cparams("arbitrary", "arbitrary"),
            name="gqa_ctx" if ctx_queries else "gqa_lat",
        )(zq, kv, kv)
    return pl.pallas_call(
        functools.partial(_diff_kernel, kv_chunk=kc, out_scale=out_scale),
        grid=grid,
        in_specs=[pl.BlockSpec(memory_space=pltpu.SMEM), q_spec] + kv_specs
        + [pl.BlockSpec((1, gw), const), pl.BlockSpec((gw, gw), const)],
        out_specs=out_spec, out_shape=out_shape,
        scratch_shapes=scratch,
        compiler_params=_cparams("arbitrary", "arbitrary"),
        name="diff_ctx" if ctx_queries else "diff_lat",
    )(lam, zq, kv, kv, norm_g, seg)


def _out_proj_kernel(rf_ref, rb_ref, rg_ref, ga_ref, hf_ref, hb_ref, hg_ref, df_ref, x_ref, mod_ref,
                     w_ref, rn_ref, hn_ref, seg_ref, n2_ref, rw_ref, rbias_ref,
                     xo_ref, h2_ref, e_ref, gt_ref):
    d = x_ref.shape[-1]
    gw = GROUP_WIDTH
    seg = seg_ref[...]
    mod = mod_ref[0]

    def readout(o, norm_g, gate):
        ms = jnp.dot(o * o, seg, precision=HIGHEST, preferred_element_type=F32)
        return o * lax.rsqrt(ms + NORM_EPS) * norm_g * (gate * _sigmoid(gate))

    parts = (readout(rf_ref[...] + rb_ref[...], rn_ref[...], rg_ref[...]),
             ga_ref[...],
             readout(hf_ref[...] + hb_ref[...], hn_ref[...], hg_ref[...]),
             df_ref[...])
    acc = jnp.zeros(x_ref.shape, F32)
    for n, part in enumerate(parts):
        acc = acc + jnp.dot(part.astype(BF16), w_ref[n * gw:(n + 1) * gw, :],
                            preferred_element_type=F32)
    x = x_ref[...] + mod[:, 2 * d:3 * d] * acc
    xo_ref[...] = x
    ms = jnp.mean(x * x, axis=-1, keepdims=True)
    h2 = x * lax.rsqrt(ms + NORM_EPS) * n2_ref[...]
    h2 = h2 * (1.0 + mod[:, 4 * d:5 * d]) + mod[:, 3 * d:4 * d]
    h2_ref[...] = h2

    logits = lax.dot_general(rw_ref[...], h2, NT_DIMS, precision=HIGHEST, preferred_element_type=F32)
    score = _sigmoid(logits)
    sel = score + rbias_ref[:, 0:1]
    ng = N_EXPERT_GROUPS
    sc = [score[m * ng:(m + 1) * ng, :] for m in range(EXPERTS_PER_GROUP)]
    sl = [sel[m * ng:(m + 1) * ng, :] for m in range(EXPERTS_PER_GROUP)]
    hi1, lo1 = jnp.maximum(sl[0], sl[1]), jnp.minimum(sl[0], sl[1])
    hi2, lo2 = jnp.maximum(sl[2], sl[3]), jnp.minimum(sl[2], sl[3])
    group_score = jnp.maximum(hi1, hi2) + jnp.maximum(jnp.minimum(hi1, hi2), jnp.maximum(lo1, lo2))
    g_iota = lax.broadcasted_iota(jnp.int32, group_score.shape, 0).astype(F32)
    g_max = jnp.max(group_score, axis=0, keepdims=True)
    best = jnp.min(jnp.where(group_score == g_max, g_iota, float(ng)), axis=0, keepdims=True)
    pick = g_iota == best
    v = [jnp.sum(jnp.where(pick, a, 0.0), axis=0, keepdims=True) for a in sl]
    u = [jnp.sum(jnp.where(pick, a, 0.0), axis=0, keepdims=True) for a in sc]
    rank = []
    for i in range(EXPERTS_PER_GROUP):
        r_i = jnp.zeros(best.shape, F32)
        for j in range(EXPERTS_PER_GROUP):
            if j < i:
                r_i = r_i + jnp.where(v[j] >= v[i], 1.0, 0.0)
            elif j > i:
                r_i = r_i + jnp.where(v[j] > v[i], 1.0, 0.0)
        rank.append(r_i)
    idx, gate = [], []
    for kth in range(2):
        idx.append(sum(jnp.where(rank[i] == kth, float(i), 0.0) for i in range(EXPERTS_PER_GROUP)))
        gate.append(sum(jnp.where(rank[i] == kth, u[i], 0.0) for i in range(EXPERTS_PER_GROUP)))
    tot = gate[0] + gate[1]
    e_ref[0:1, :] = (best * EXPERTS_PER_GROUP + idx[0]).astype(jnp.int32)
    e_ref[1:2, :] = (best * EXPERTS_PER_GROUP + idx[1]).astype(jnp.int32)
    gt_ref[0:1, :] = gate[0] / tot
    gt_ref[1:2, :] = gate[1] / tot


def _out_proj_call(rf, rb, zr, o_gqa, hf, hb, zh, o_df, x_all, mod_all, w_out_bf, rn, hn, seg, n2,
                   rw_t, rbias, n_rows, n_lat_rows, t_len):
    d = x_all.shape[-1]
    tm = ROW_TILE
    gw = GROUP_WIDTH
    n_lat_tiles = n_lat_rows // tm
    tiles_per_seq = t_len // tm
    n_batch = n_lat_rows // t_len

    def mod_idx(i):
        return (jnp.where(i < n_lat_tiles, i // tiles_per_seq, n_batch), 0, 0)

    row = lambda i: (i, 0)
    const = lambda i: (0, 0)
    blk = lambda col: pl.BlockSpec((tm, gw), lambda i: (i, col))
    return pl.pallas_call(
        _out_proj_kernel,
        grid=(n_rows // tm,),
        in_specs=[blk(0), blk(0), blk(3), blk(0), blk(0), blk(0), blk(4), blk(0),
                  pl.BlockSpec((tm, d), row),
                  pl.BlockSpec((1, 1, mod_all.shape[-1]), mod_idx),
                  pl.BlockSpec((4 * gw, d), const),
                  pl.BlockSpec((1, gw), const), pl.BlockSpec((1, gw), const),
                  pl.BlockSpec((gw, gw), const),
                  pl.BlockSpec((1, d), const),
                  pl.BlockSpec((N_EXPERTS, d), const),
                  pl.BlockSpec((N_EXPERTS, 128), const)],
        out_specs=[pl.BlockSpec((tm, d), row), pl.BlockSpec((tm, d), row),
                   pl.BlockSpec((2, tm), lambda i: (0, i)), pl.BlockSpec((2, tm), lambda i: (0, i))],
        out_shape=[jax.ShapeDtypeStruct((n_rows, d), F32), jax.ShapeDtypeStruct((n_rows, d), F32),
                   jax.ShapeDtypeStruct((2, n_rows), jnp.int32),
                   jax.ShapeDtypeStruct((2, n_rows), F32)],
        compiler_params=_cparams("arbitrary"),
        name="out_proj",
    )(rf, rb, zr, o_gqa, hf, hb, zh, o_df, x_all, mod_all, w_out_bf, rn, hn, seg, n2, rw_t, rbias)


def _expert_kernel(be_ref, nb_ref, idx_hbm, gate_ref, h2_hbm, wg_ref, wu_ref, wd_ref, ys_hbm,
                   idx_ref, x_ref, y_ref, sem_i, sem_g, sem_s):
    i = pl.program_id(0)
    nb = nb_ref[0]
    rows = MOE_BLOCK

    def idx_copy(blk):
        return pltpu.make_async_copy(idx_hbm.at[blk], idx_ref.at[blk % 3], sem_i.at[blk % 3])

    def issue_gather(blk):
        islot, xslot = blk % 3, blk % 2
        for r in range(rows):
            pltpu.make_async_copy(h2_hbm.at[pl.ds(idx_ref[islot, r], 1)],
                                  x_ref.at[xslot, pl.ds(r, 1)], sem_g.at[xslot]).start()

    def wait_slot(buf_ref, sem, slot):
        pltpu.make_async_copy(buf_ref.at[slot], buf_ref.at[slot], sem.at[slot]).wait()

    @pl.when(i < nb)
    def _():
        slot = i % 2

        @pl.when(i == 0)
        def _():
            y_ref[...] = jnp.zeros(y_ref.shape, F32)
            n_real = ys_hbm.shape[0] - 2 * rows
            for s in range(2):
                fill = pltpu.make_async_copy(y_ref.at[s], ys_hbm.at[pl.ds(n_real + s * rows, rows)],
                                             sem_s.at[s])
                fill.start()
                fill.wait()
            idx_copy(0).start()
            idx_copy(0).wait()
            issue_gather(0)

            @pl.when(nb > 1)
            def _():
                idx_copy(1).start()

        @pl.when(i + 1 < nb)
        def _():
            idx_copy(i + 1).wait()
            issue_gather(i + 1)

            @pl.when(i + 2 < nb)
            def _():
                idx_copy(i + 2).start()

        wait_slot(x_ref, sem_g, slot)

        @pl.when(i >= 2)
        def _():
            wait_slot(y_ref, sem_s, slot)

        xb = x_ref[slot].astype(BF16)
        a = jnp.dot(xb, wg_ref[0], preferred_element_type=F32)
        u = jnp.dot(xb, wu_ref[0], preferred_element_type=F32)
        hmid = (a * _sigmoid(a) * u).astype(BF16)
        y_ref[slot] = jnp.dot(hmid, wd_ref[0], preferred_element_type=F32) * gate_ref[:, 0:1]

        islot = i % 3
        for r in range(rows):
            pltpu.make_async_copy(y_ref.at[slot, pl.ds(r, 1)],
                                  ys_hbm.at[pl.ds(idx_ref[islot, rows + r], 1)], sem_s.at[slot]).start()

        @pl.when(i == nb - 1)
        def _():
            @pl.when(i >= 1)
            def _():
                wait_slot(y_ref, sem_s, 1 - slot)
            wait_slot(y_ref, sem_s, slot)


def _expert_call(block_expert, n_used, row_idx, row_gate, h2, wg, wu, wd, n_tok):
    n_blocks = block_expert.shape[0]
    d = h2.shape[-1]
    ff = wg.shape[-1]
    rows = MOE_BLOCK
    grid_spec = pltpu.PrefetchScalarGridSpec(
        num_scalar_prefetch=2,
        grid=(n_blocks,),
        in_specs=[pl.BlockSpec(memory_space=pl.ANY),
                  pl.BlockSpec((rows, 1), lambda i, be, nb: (i, 0)),
                  pl.BlockSpec(memory_space=pl.ANY),
                  pl.BlockSpec((1, d, ff), lambda i, be, nb: (be[i], 0, 0)),
                  pl.BlockSpec((1, d, ff), lambda i, be, nb: (be[i], 0, 0)),
                  pl.BlockSpec((1, ff, d), lambda i, be, nb: (be[i], 0, 0))],
        out_specs=pl.BlockSpec(memory_space=pl.ANY),
        scratch_shapes=[pltpu.SMEM((3, 2 * rows), jnp.int32),
                        pltpu.VMEM((2, rows, d), F32),
                        pltpu.VMEM((2, rows, d), F32),
                        pltpu.SemaphoreType.DMA((3,)), pltpu.SemaphoreType.DMA((2,)),
                        pltpu.SemaphoreType.DMA((2,))],
    )
    return pl.pallas_call(
        _expert_kernel,
        grid_spec=grid_spec,
        out_shape=jax.ShapeDtypeStruct((2 * n_tok + 2 * rows, d), F32),
        compiler_params=_cparams("arbitrary"),
        name="experts",
    )(block_expert, n_used, row_idx, row_gate, h2, wg, wu, wd)


def _dispatch_plan(e_t, g_t, n_tok):
    m = 2 * n_tok
    e_flat = e_t.reshape(m)
    order = jnp.argsort(e_flat).astype(jnp.int32)
    experts = jnp.arange(N_EXPERTS, dtype=jnp.int32)
    counts = jnp.sum((e_flat[:, None] == experts[None, :]).astype(jnp.int32), axis=0)
    padded = (counts + MOE_BLOCK - 1) // MOE_BLOCK * MOE_BLOCK
    start = jnp.cumsum(counts) - counts
    pend = jnp.cumsum(padded)
    pstart = pend - padded
    n_blocks = -(-m // MOE_BLOCK) + N_EXPERTS
    blk_row0 = jnp.arange(n_blocks, dtype=jnp.int32) * MOE_BLOCK
    block_expert = jnp.minimum(
        jnp.sum((pend[None, :] <= blk_row0[:, None]).astype(jnp.int32), axis=1), N_EXPERTS - 1)
    blk = jnp.arange(n_blocks, dtype=jnp.int32)[:, None]
    r_in = jnp.arange(MOE_BLOCK, dtype=jnp.int32)[None, :]
    pos = blk_row0[:, None] + r_in - pstart[block_expert][:, None]
    valid = pos < counts[block_expert][:, None]
    sorted_pos = jnp.clip(start[block_expert][:, None] + pos, 0, m - 1)
    assign = order[sorted_pos]
    dump = m + (blk % 2) * MOE_BLOCK + r_in
    row_src = jnp.where(valid, assign % n_tok, 0)
    row_dst = jnp.where(valid, assign, dump)
    row_idx = jnp.concatenate([row_src, row_dst], axis=1).astype(jnp.int32)
    row_gate = jnp.where(valid, g_t.reshape(m)[assign], 0.0)
    n_used = (pend[-1] // MOE_BLOCK).astype(jnp.int32).reshape(1)
    return block_expert.astype(jnp.int32), n_used, row_idx, row_gate.reshape(n_blocks * MOE_BLOCK, 1)


def _combine_kernel(x_ref, y0_ref, y1_ref, mod_ref, fg_ref, o_ref, *, final_norm):
    d = x_ref.shape[-1]
    mod = mod_ref[0]
    x = x_ref[...] + mod[:, 5 * d:6 * d] * (y0_ref[...] + y1_ref[...])
    if final_norm:
        ms = jnp.mean(x * x, axis=-1, keepdims=True)
        x = x * lax.rsqrt(ms + NORM_EPS) * fg_ref[...]
    o_ref[...] = x


def _combine_call(x_new, ys, mod_all, final_g, n_rows, n_lat_rows, t_len, final_norm):
    d = x_new.shape[-1]
    tm = ROW_TILE
    n_lat_tiles = n_lat_rows // tm
    tiles_per_seq = t_len // tm
    n_batch = n_lat_rows // t_len
    n_tiles = n_rows // tm

    def mod_idx(i):
        return (jnp.where(i < n_lat_tiles, i // tiles_per_seq, n_batch), 0, 0)

    return pl.pallas_call(
        functools.partial(_combine_kernel, final_norm=final_norm),
        grid=(n_tiles,),
        in_specs=[pl.BlockSpec((tm, d), lambda i: (i, 0)),
                  pl.BlockSpec((tm, d), lambda i: (i, 0)),
                  pl.BlockSpec((tm, d), lambda i: (i + n_tiles, 0)),
                  pl.BlockSpec((1, 1, mod_all.shape[-1]), mod_idx),
                  pl.BlockSpec((1, d), lambda i: (0, 0))],
        out_specs=pl.BlockSpec((tm, d), lambda i: (i, 0)),
        out_shape=jax.ShapeDtypeStruct((n_rows, d), F32),
        compiler_params=_cparams("arbitrary"),
        name="combine",
    )(x_new, ys, ys, mod_all, final_g)


def _rope_tables(t_len, rot_dim, pad_rows):
    n_freq = rot_dim // 4
    half = rot_dim // 2
    inv_freq = ROPE_BASE ** (-jnp.arange(n_freq, dtype=F32) / n_freq)
    pos = jnp.arange(t_len)
    row = (pos // GRID_W).astype(F32)
    col = (pos % GRID_W).astype(F32)
    ang = jnp.concatenate([row[:, None] * inv_freq, col[:, None] * inv_freq], axis=-1)
    cos, sin = jnp.cos(ang), jnp.sin(ang)
    reps = GROUP_WIDTH // rot_dim
    zeros = jnp.zeros_like(sin)
    cos_f = jnp.tile(jnp.concatenate([cos, cos], -1), (1, reps))
    sin_a = jnp.tile(jnp.concatenate([-sin, zeros], -1), (1, reps))
    sin_b = jnp.tile(jnp.concatenate([zeros, sin], -1), (1, reps))
    pad1 = jnp.ones((pad_rows, GROUP_WIDTH), F32)
    pad0 = jnp.zeros((pad_rows, GROUP_WIDTH), F32)
    return (jnp.concatenate([cos_f, pad1], 0), jnp.concatenate([sin_a, pad0], 0),
            jnp.concatenate([sin_b, pad0], 0))


def kernel(x, c, ctx, c_ctx, w_mod, b_mod, norm1_g, norm2_g, w_in, ret_decay_logit, ret_norm_g,
           gqa_qnorm_g, gqa_knorm_g, hgrn_lb_logit, hgrn_norm_g, diff_lambda, diff_norm_g, w_out,
           router_w, router_bias, moe_w_gate, moe_w_up, moe_w_down, final_norm_g):
    n_batch, t_len, d = x.shape
    ctx_len = ctx.shape[1]
    depth = w_mod.shape[0]
    gw = GROUP_WIDTH
    lat_rows = n_batch * t_len
    all_rows = lat_rows + n_batch * ctx_len
    assert t_len % ROW_TILE == 0 and ctx_len % ROW_TILE == 0 and t_len % ATT_Q_TILE == 0
    assert ctx_len % REC_CHUNK == 0 and (ctx_len + t_len) % ctx_len == 0 and n_batch + 1 <= 8

    x_all = jnp.concatenate([x.reshape(lat_rows, d), ctx.reshape(n_batch * ctx_len, d)], axis=0)
    cvec = jnp.zeros((8, d), F32).at[:n_batch].set(c).at[n_batch].set(c_ctx)

    tabs = _rope_tables(t_len, HEAD_DIM, ROW_TILE) + _rope_tables(t_len, DIFF_QK_DIM, ROW_TILE)
    head_of = jnp.arange(gw) // HEAD_DIM
    same_head = head_of[:, None] == head_of[None, :]
    seg = same_head.astype(F32) / HEAD_DIM
    bd = same_head.astype(BF16)

    sm = jax.nn.softmax(hgrn_lb_logit.astype(F32), axis=1)
    lower_bounds = jnp.cumsum(sm, axis=1) - sm[:, :1]
    log_gamma = jax.nn.log_sigmoid(ret_decay_logit.astype(F32))

    perm = (jnp.arange(N_EXPERTS) % N_EXPERT_GROUPS) * EXPERTS_PER_GROUP + jnp.arange(N_EXPERTS) // N_EXPERT_GROUPS
    rw_t = router_w.T[perm]
    rbias = jnp.broadcast_to(router_bias.astype(F32)[perm][:, None], (N_EXPERTS, 128))

    tile = lambda g: jnp.tile(g.astype(F32), GROUP_HEADS).reshape(1, gw)

    out = None
    for layer in range(depth):
        need_ctx = layer < depth - 1
        lambda_init = 0.8 - 0.6 * math.exp(-0.3 * layer)
        mod_all = _mod_call(cvec, w_mod[layer], b_mod[layer]).reshape(8, 1, 6 * d)

        w_l = w_in[layer]
        w_bf = w_l.astype(BF16)
        wvt_bf = jnp.concatenate([w_l[:, 2 * gw:3 * gw], w_l[:, 9 * gw:10 * gw]], axis=1).T.astype(BF16)
        zr, zg, zh, zd, vt_all, kvg, kvd = _in_proj_call(
            x_all, mod_all, norm1_g[layer].reshape(1, d), w_bf, wvt_bf, tabs,
            tile(gqa_qnorm_g[layer]), tile(gqa_knorm_g[layer])[:, :128], seg, lat_rows, t_len)

        rec_kw = dict(n_batch=n_batch, t_len=t_len, ctx_len=ctx_len)
        ret_par = jnp.repeat(log_gamma[layer], HEAD_DIM, axis=-1)
        rf, rb = _rec_call(zr, vt_all, ret_par, bd, hgrn=False, cols=(0, 1, 1, 2), vt_row=0, **rec_kw)
        hf, hb = _rec_call(zh, vt_all, lower_bounds[:, layer], bd, hgrn=True, cols=(0, 1, 2, 3),
                           vt_row=1, **rec_kw)

        lp = diff_lambda[layer].astype(F32)
        lam = (jnp.exp(jnp.sum(lp[0] * lp[1])) - jnp.exp(jnp.sum(lp[2] * lp[3])) + lambda_init)
        lam = lam.reshape(1, 1)
        att_kw = dict(n_batch=n_batch, t_len=t_len, ctx_len=ctx_len)
        diff_kw = dict(lam=lam, norm_g=tile(diff_norm_g[layer]), seg=seg, out_scale=1.0 - lambda_init)
        o_gqa = _attn_call("gqa", zg, kvg, ctx_queries=False, **att_kw)
        o_df = _attn_call("diff", zd, kvd, ctx_queries=False, **att_kw, **diff_kw)
        n_rows = lat_rows
        if need_ctx:
            o_gqa = jnp.concatenate([o_gqa, _attn_call("gqa", zg, kvg, ctx_queries=True, **att_kw)], 0)
            o_df = jnp.concatenate(
                [o_df, _attn_call("diff", zd, kvd, ctx_queries=True, **att_kw, **diff_kw)], 0)
            n_rows = all_rows

        x_new, h2, e_t, g_t = _out_proj_call(
            rf, rb, zr, o_gqa, hf, hb, zh, o_df, x_all, mod_all, w_out[layer].astype(BF16),
            tile(ret_norm_g[layer]), tile(hgrn_norm_g[layer]), seg, norm2_g[layer].reshape(1, d),
            rw_t, rbias, n_rows, lat_rows, t_len)

        plan = _dispatch_plan(e_t, g_t, n_rows)
        ys = _expert_call(*plan, h2, moe_w_gate[layer].astype(BF16), moe_w_up[layer].astype(BF16),
                          moe_w_down[layer].astype(BF16), n_rows)
        last = layer == depth - 1
        x_next = _combine_call(x_new, ys, mod_all, final_norm_g.reshape(1, d), n_rows, lat_rows, t_len,
                               final_norm=last)
        if last:
            out = x_next[:lat_rows].reshape(n_batch, t_len, d)
        else:
            x_all = x_next
    return out
```

```python
import functools
import math

import jax
import jax.numpy as jnp
from jax import lax
from jax.experimental import pallas as pl
from jax.experimental.pallas import tpu as pltpu

F32 = jnp.float32
BF16 = jnp.bfloat16
HIGHEST = lax.Precision.HIGHEST

HEAD_DIM = 64
GROUP_HEADS = 4
GROUP_WIDTH = GROUP_HEADS * HEAD_DIM
GQA_KV_HEADS = 2
DIFF_QK_DIM = HEAD_DIM // 2
GRID_W = 64
ROPE_BASE = 10000.0
N_EXPERTS = 32
N_EXPERT_GROUPS = 8
EXPERTS_PER_GROUP = 4
MOE_BLOCK = 256
NORM_EPS = 1e-6

ROW_TILE = 256
REC_CHUNK = 128
REC_SUB = 16
ATT_Q_TILE = 256
ATT_KV_CHUNK_MAX = 1536
VMEM_LIMIT = 56 * 1024 * 1024

NT_DIMS = (((1,), (1,)), ((), ()))
LOG2E = math.log2(math.e)


def _sigmoid(v):
    return 1.0 / (1.0 + jnp.exp(-v))


def _cparams(*sem):
    return pltpu.CompilerParams(dimension_semantics=sem, vmem_limit_bytes=VMEM_LIMIT)


def _mod_kernel(c_ref, w_ref, b_ref, o_ref):
    cv = c_ref[...]
    s = cv * _sigmoid(cv)
    o_ref[...] = jnp.dot(s, w_ref[...], precision=HIGHEST, preferred_element_type=F32) + b_ref[...]


def _mod_call(cvec, w_mod, b_mod):
    d, n = w_mod.shape
    tn = 1024
    return pl.pallas_call(
        _mod_kernel,
        grid=(n // tn,),
        in_specs=[pl.BlockSpec((8, d), lambda j: (0, 0)),
                  pl.BlockSpec((d, tn), lambda j: (0, j)),
                  pl.BlockSpec((1, tn), lambda j: (0, j))],
        out_specs=pl.BlockSpec((8, tn), lambda j: (0, j)),
        out_shape=jax.ShapeDtypeStruct((8, n), F32),
        compiler_params=_cparams("arbitrary"),
        name="mod",
    )(cvec, w_mod, b_mod.reshape(1, n))


def _rope(v, cos, sin_a, sin_b, half):
    n = v.shape[-1]
    return v * cos + pltpu.roll(v, n - half, 1) * sin_a + pltpu.roll(v, half, 1) * sin_b


def _in_proj_kernel(x_ref, mod_ref, g1_ref, w_ref, wvt_ref, c64_ref, sa64_ref, sb64_ref,
                    c32_ref, sa32_ref, sb32_ref, qg_ref, kg_ref, seg_ref,
                    zr_ref, zg_ref, zh_ref, zd_ref, vt_ref, kvg_ref, kvd_ref):
    d = x_ref.shape[-1]
    gw = GROUP_WIDTH
    x = x_ref[...]
    ms = jnp.mean(x * x, axis=-1, keepdims=True)
    h = x * lax.rsqrt(ms + NORM_EPS) * g1_ref[...]
    mod = mod_ref[0]
    h = h * (1.0 + mod[:, d:2 * d]) + mod[:, 0:d]
    hb = h.astype(BF16)

    def proj(c0, c1):
        return jnp.dot(hb, w_ref[:, c0:c1], preferred_element_type=F32)

    c64, sa64, sb64 = c64_ref[...], sa64_ref[...], sb64_ref[...]
    c32, sa32, sb32 = c32_ref[...], sa32_ref[...], sb32_ref[...]
    seg = seg_ref[...]

    zr_ref[:, 0:gw] = _rope(proj(0, gw) * HEAD_DIM ** -0.5, c64, sa64, sb64, 32)
    zr_ref[:, gw:2 * gw] = _rope(proj(gw, 2 * gw), c64, sa64, sb64, 32)
    zr_ref[:, 2 * gw:4 * gw] = proj(2 * gw, 4 * gw)

    o = 4 * gw
    q = proj(o, o + gw)
    qms = jnp.dot(q * q, seg, precision=HIGHEST, preferred_element_type=F32)
    q = q * lax.rsqrt(qms + NORM_EPS) * qg_ref[...]
    q = _rope(q, c64, sa64, sb64, 32) * (HEAD_DIM ** -0.5 * LOG2E)
    lane = lax.broadcasted_iota(jnp.int32, (q.shape[0], 128), 1)
    lo = lane < HEAD_DIM
    qa, qb = q[:, 0:128], q[:, 128:256]
    zg_ref[:, 0:128] = jnp.where(lo, qa, 0.0).astype(BF16)
    zg_ref[:, 128:256] = jnp.where(lo, pltpu.roll(qa, 64, 1), 0.0).astype(BF16)
    zg_ref[:, 256:384] = jnp.where(lo, 0.0, pltpu.roll(qb, 64, 1)).astype(BF16)
    zg_ref[:, 384:512] = jnp.where(lo, 0.0, qb).astype(BF16)
    k = proj(o + gw, o + gw + 128)
    kms = jnp.dot(k * k, seg[0:128, 0:128], precision=HIGHEST, preferred_element_type=F32)
    k = k * lax.rsqrt(kms + NORM_EPS) * kg_ref[...]
    kvg_ref[:, 0:128] = _rope(k, c64[:, 0:128], sa64[:, 0:128], sb64[:, 0:128], 32).astype(BF16)
    kvg_ref[:, 128:256] = proj(o + gw + 128, o + gw + 256).astype(BF16)

    o = 6 * gw
    zh_ref[...] = proj(o, o + 5 * gw)

    o = 11 * gw
    zd_ref[...] = (_rope(proj(o, o + gw), c32, sa32, sb32, 16) * (DIFF_QK_DIM ** -0.5 * LOG2E)).astype(BF16)
    kvd_ref[:, 0:gw] = _rope(proj(o + gw, o + 2 * gw), c32, sa32, sb32, 16).astype(BF16)
    kvd_ref[:, gw:2 * gw] = proj(o + 2 * gw, o + 3 * gw).astype(BF16)

    vt_ref[...] = lax.dot_general(wvt_ref[...], hb, NT_DIMS, preferred_element_type=F32).astype(BF16)


def _in_proj_call(x_all, mod_all, g1, w_bf, wvt_bf, tabs, qg, kg, seg, n_lat_rows, t_len):
    r, d = x_all.shape
    tm = ROW_TILE
    gw = GROUP_WIDTH
    n_lat_tiles = n_lat_rows // tm
    tiles_per_seq = t_len // tm
    n_batch = n_lat_rows // t_len

    def mod_idx(i):
        return (jnp.where(i < n_lat_tiles, i // tiles_per_seq, n_batch), 0, 0)

    def tab_idx(i):
        return (jnp.where(i < n_lat_tiles, i % tiles_per_seq, tiles_per_seq), 0)

    ctx_tiles = (r - n_lat_rows) // n_batch // tm
    seq_tiles = tiles_per_seq + ctx_tiles

    def kv_idx(i):
        ci = i - n_lat_tiles
        lat = (i // tiles_per_seq) * seq_tiles + ctx_tiles + i % tiles_per_seq
        ctx = (ci // ctx_tiles) * seq_tiles + ci % ctx_tiles
        return (jnp.where(i < n_lat_tiles, lat, ctx), 0)

    row = lambda i: (i, 0)
    const = lambda i: (0, 0)
    tab_spec = pl.BlockSpec((tm, gw), tab_idx)
    return pl.pallas_call(
        _in_proj_kernel,
        grid=(r // tm,),
        in_specs=[pl.BlockSpec((tm, d), row),
                  pl.BlockSpec((1, 1, mod_all.shape[-1]), mod_idx),
                  pl.BlockSpec((1, d), const),
                  pl.BlockSpec(w_bf.shape, const),
                  pl.BlockSpec(wvt_bf.shape, const),
                  tab_spec, tab_spec, tab_spec, tab_spec, tab_spec, tab_spec,
                  pl.BlockSpec((1, gw), const),
                  pl.BlockSpec((1, 128), const),
                  pl.BlockSpec((gw, gw), const)],
        out_specs=[pl.BlockSpec((tm, 4 * gw), row),
                   pl.BlockSpec((tm, 2 * gw), row),
                   pl.BlockSpec((tm, 5 * gw), row),
                   pl.BlockSpec((tm, gw), row),
                   pl.BlockSpec((2 * gw, tm), lambda i: (0, i)),
                   pl.BlockSpec((tm, gw), kv_idx),
                   pl.BlockSpec((tm, 2 * gw), kv_idx)],
        out_shape=[jax.ShapeDtypeStruct((r, 4 * gw), F32),
                   jax.ShapeDtypeStruct((r, 2 * gw), BF16),
                   jax.ShapeDtypeStruct((r, 5 * gw), F32),
                   jax.ShapeDtypeStruct((r, gw), BF16),
                   jax.ShapeDtypeStruct((2 * gw, r), BF16),
                   jax.ShapeDtypeStruct((r, gw), BF16),
                   jax.ShapeDtypeStruct((r, 2 * gw), BF16)],
        compiler_params=_cparams("arbitrary"),
        name="in_proj",
    )(x_all, mod_all, g1, w_bf, wvt_bf, *tabs, qg, kg, seg)


def _rec_direction(q_ref, k_ref, v_ref, vt_ref, par, bd_ref, o_ref, s_ref,
                   lc_ref, qt_ref, kh_ref, kk_ref, tot_ref, a_ref, *, hgrn, reverse):
    c, gw = q_ref.shape
    sub = REC_SUB
    n_sub = c // sub
    kin = k_ref[...]
    if hgrn:
        sig = _sigmoid(kin)
        logf = jnp.log(par + (1.0 - par) * sig)
        kk = (1.0 - par) * (1.0 - sig)
    else:
        logf = jnp.broadcast_to(par, (c, gw))
        kk = kin
    r_i = lax.broadcasted_iota(jnp.int32, (c, c), 0)
    c_i = lax.broadcasted_iota(jnp.int32, (c, c), 1)
    shift = sub.bit_length() - 1
    same = jnp.right_shift(r_i, shift) == jnp.right_shift(c_i, shift)
    if reverse:
        incl = same & (c_i >= r_i)
        excl = same & (c_i < r_i)
    else:
        incl = same & (c_i <= r_i)
        excl = same & (c_i > r_i)
    lc = jnp.dot(jnp.where(incl, 1.0, 0.0), logf, precision=HIGHEST, preferred_element_type=F32)
    rr = jnp.dot(jnp.where(excl, 1.0, 0.0), logf, precision=HIGHEST, preferred_element_type=F32)
    lc_ref[...] = lc
    qt_ref[...] = (q_ref[...] * jnp.exp(lc)).astype(BF16)
    kh_ref[...] = kk * jnp.exp(rr)
    kk_ref[...] = kk
    tot_ref[...] = lc + rr
    vt = vt_ref[...]
    bd = bd_ref[...]
    t_loc = lax.broadcasted_iota(jnp.int32, (sub, gw), 0)
    row_id = lax.broadcasted_iota(jnp.int32, (c, gw), 0)
    head_shift = HEAD_DIM.bit_length() - 1
    same_head = (jnp.right_shift(lax.broadcasted_iota(jnp.int32, (gw, gw), 0), head_shift)
                 == jnp.right_shift(lax.broadcasted_iota(jnp.int32, (gw, gw), 1), head_shift))

    def body(j, carry):
        a = (n_sub - 1 - j) if reverse else j
        base = pl.multiple_of(a * sub, sub)
        rows = pl.ds(base, sub)
        s_t = s_ref[...]
        o_inter = lax.dot_general(qt_ref[rows, :], s_t.astype(BF16), NT_DIMS,
                                  preferred_element_type=F32)
        lc_a = lc_ref[rows, :]
        q_a = q_ref[rows, :]
        k_a = kk_ref[rows, :]
        v_a = v_ref[rows, :]
        for s in range(sub):
            valid = (t_loc <= s) if reverse else (t_loc >= s)
            e = jnp.where(valid, jnp.exp(lc_a - lc_a[s:s + 1, :]), 0.0)
            a_ref[s * sub:(s + 1) * sub, :] = (q_a * e * k_a[s:s + 1, :]).astype(BF16)
        b = jnp.dot(a_ref[...], bd, preferred_element_type=F32)
        o_intra = jnp.zeros((sub, gw), F32)
        for s in range(sub):
            o_intra = o_intra + b[s * sub:(s + 1) * sub, :] * v_a[s:s + 1, :]
        o_ref[rows, :] = o_inter + o_intra
        in_sub = (row_id >= base) & (row_id < base + sub)
        kh_m = jnp.where(in_sub, kh_ref[...], 0.0).astype(BF16)
        u = jnp.dot(vt, kh_m, preferred_element_type=F32)
        decay = jnp.exp(tot_ref[pl.ds(base, 1), :])
        s_ref[...] = s_t * decay + jnp.where(same_head, u, 0.0)
        return carry

    lax.fori_loop(0, n_sub, body, 0)


def _rec_kernel(qf_ref, kf_ref, vf_ref, vtf_ref, qb_ref, kb_ref, vb_ref, vtb_ref, par_ref, bd_ref,
                of_ref, ob_ref, sf_ref, sb_ref, lc_ref, qt_ref, kh_ref, kk_ref, tot_ref, a_ref,
                *, hgrn):
    @pl.when(pl.program_id(1) == 0)
    def _():
        sf_ref[...] = jnp.zeros_like(sf_ref)
        sb_ref[...] = jnp.zeros_like(sb_ref)

    tmp = (lc_ref, qt_ref, kh_ref, kk_ref, tot_ref, a_ref)
    _rec_direction(qf_ref, kf_ref, vf_ref, vtf_ref, par_ref[0:1, :], bd_ref, of_ref, sf_ref, *tmp,
                   hgrn=hgrn, reverse=False)
    _rec_direction(qb_ref, kb_ref, vb_ref, vtb_ref, par_ref[1:2, :], bd_ref, ob_ref, sb_ref, *tmp,
                   hgrn=hgrn, reverse=True)


def _ret_direction(q_ref, k_ref, v_ref, vt_ref, logg, o_ref, s_ref, *, reverse):
    c, gw = q_ref.shape
    q = q_ref[...]
    k = k_ref[...]
    t_row = lax.broadcasted_iota(jnp.int32, (c, gw), 0).astype(F32)
    if reverse:
        n_q, n_k = c - t_row, t_row
    else:
        n_q, n_k = t_row + 1.0, (c - 1.0) - t_row
    s_t = s_ref[...]
    qt = (q * jnp.exp(n_q * logg)).astype(BF16)
    o = lax.dot_general(qt, s_t.astype(BF16), NT_DIMS, preferred_element_type=F32)
    r_i = lax.broadcasted_iota(jnp.int32, (c, c), 0)
    c_i = lax.broadcasted_iota(jnp.int32, (c, c), 1)
    dist = (c_i - r_i) if reverse else (r_i - c_i)
    allowed = dist >= 0
    dist_f = dist.astype(F32)
    lane = lax.broadcasted_iota(jnp.int32, (c, gw), 1)
    kb = k.astype(BF16)
    vb = v_ref[...].astype(BF16)
    for h in range(GROUP_HEADS):
        in_h = (lane >= h * HEAD_DIM) & (lane < (h + 1) * HEAD_DIM)
        mask = jnp.where(allowed, jnp.exp(dist_f * logg[:, h * HEAD_DIM:h * HEAD_DIM + 1]), 0.0)
        qh = jnp.where(in_h, q, 0.0).astype(BF16)
        sc = lax.dot_general(qh, kb, NT_DIMS, preferred_element_type=F32) * mask
        oh = jnp.dot(sc.astype(BF16), vb, preferred_element_type=F32)
        o = o + jnp.where(in_h, oh, 0.0)
    o_ref[...] = o
    kh = (k * jnp.exp(n_k * logg)).astype(BF16)
    u = jnp.dot(vt_ref[...], kh, preferred_element_type=F32)
    head_shift = HEAD_DIM.bit_length() - 1
    same_head = (jnp.right_shift(lax.broadcasted_iota(jnp.int32, (gw, gw), 0), head_shift)
                 == jnp.right_shift(lax.broadcasted_iota(jnp.int32, (gw, gw), 1), head_shift))
    s_ref[...] = s_t * jnp.exp(float(c) * logg) + jnp.where(same_head, u, 0.0)


def _ret_kernel(qf_ref, kf_ref, vf_ref, vtf_ref, qb_ref, kb_ref, vb_ref, vtb_ref, par_ref, bd_ref,
                of_ref, ob_ref, sf_ref, sb_ref):
    @pl.when(pl.program_id(1) == 0)
    def _():
        sf_ref[...] = jnp.zeros_like(sf_ref)
        sb_ref[...] = jnp.zeros_like(sb_ref)

    _ret_direction(qf_ref, kf_ref, vf_ref, vtf_ref, par_ref[0:1, :], of_ref, sf_ref, reverse=False)
    _ret_direction(qb_ref, kb_ref, vb_ref, vtb_ref, par_ref[1:2, :], ob_ref, sb_ref, reverse=True)


def _rec_call(z, vt_all, par, bd, *, hgrn, n_batch, t_len, ctx_len, cols, vt_row):
    r = z.shape[0]
    c = REC_CHUNK
    gw = GROUP_WIDTH
    n_ctx = ctx_len // c
    n_lat = t_len // c
    lat_blocks = n_batch * n_lat

    def fwd_blk(b, i):
        return jnp.where(i < n_ctx, lat_blocks + b * n_ctx + i, b * n_lat + i - n_ctx)

    def bwd_blk(b, i):
        return jnp.where(i < n_ctx, lat_blocks + b * n_ctx + (n_ctx - 1 - i),
                         b * n_lat + (n_lat - 1 - (i - n_ctx)))

    def zspec(blk, col):
        return pl.BlockSpec((c, gw), lambda b, i: (blk(b, i), col))

    def vtspec(blk):
        return pl.BlockSpec((gw, c), lambda b, i: (vt_row, blk(b, i)))

    const = lambda b, i: (0, 0)
    cq, ckf, ckb, cv = cols
    state_scratch = [pltpu.VMEM((gw, gw), F32), pltpu.VMEM((gw, gw), F32)]
    if hgrn:
        body = functools.partial(_rec_kernel, hgrn=True)
        scratch = state_scratch + [
            pltpu.VMEM((c, gw), F32), pltpu.VMEM((c, gw), BF16), pltpu.VMEM((c, gw), F32),
            pltpu.VMEM((c, gw), F32), pltpu.VMEM((c, gw), F32),
            pltpu.VMEM((REC_SUB * REC_SUB, gw), BF16)]
    else:
        body = _ret_kernel
        scratch = state_scratch
    out_f, out_b = pl.pallas_call(
        body,
        grid=(n_batch, n_ctx + n_lat),
        in_specs=[zspec(fwd_blk, cq), zspec(fwd_blk, ckf), zspec(fwd_blk, cv), vtspec(fwd_blk),
                  zspec(bwd_blk, cq), zspec(bwd_blk, ckb), zspec(bwd_blk, cv), vtspec(bwd_blk),
                  pl.BlockSpec((2, gw), const),
                  pl.BlockSpec((gw, gw), const)],
        out_specs=[pl.BlockSpec((c, gw), lambda b, i: (fwd_blk(b, i), 0)),
                   pl.BlockSpec((c, gw), lambda b, i: (bwd_blk(b, i), 0))],
        out_shape=[jax.ShapeDtypeStruct((r, gw), F32), jax.ShapeDtypeStruct((r, gw), F32)],
        scratch_shapes=scratch,
        compiler_params=_cparams("arbitrary", "arbitrary"),
        name="hgrn_rec" if hgrn else "ret_rec",
    )(z, z, z, vt_all, z, z, z, vt_all, par, bd)
    return out_f, out_b


def _online_softmax(lhs_ref, k_ref, v_ref, kv_chunk, scr):
    m_ref, l_ref, acc_ref = scr[0:3]
    s_refs, p_refs, a_refs, x_refs = scr[3:5], scr[5:7], scr[7:9], scr[9:11]
    n_heads = lhs_ref.shape[0]
    n = k_ref.shape[0] // kv_chunk
    steps = n_heads * n
    assert steps % 2 == 0 and steps >= 4
    m_ref[...] = jnp.full(m_ref.shape, -jnp.inf, F32)
    l_ref[...] = jnp.zeros(l_ref.shape, F32)
    acc_ref[...] = jnp.zeros(acc_ref.shape, F32)

    def split(t):
        if isinstance(t, int):
            return t // n, pl.ds((t % n) * kv_chunk, kv_chunk)
        h = t // n
        return h, pl.ds(pl.multiple_of((t - h * n) * kv_chunk, kv_chunk), kv_chunk)

    def qk(t, slot):
        h, rows = split(t)
        s = lax.dot_general(lhs_ref[h], k_ref[rows, :], NT_DIMS, preferred_element_type=F32)
        s_refs[slot][...] = s
        x_refs[slot][...] = jnp.max(s, axis=-1, keepdims=True)

    def sm(t, slot):
        h, _ = split(t)
        s = s_refs[slot][...]
        m = m_ref[h]
        m_new = jnp.maximum(m, x_refs[slot][...])
        alpha = jnp.exp2(m - m_new)
        p = jnp.exp2(s - m_new)
        m_ref[h] = m_new
        l_ref[h] = alpha * l_ref[h] + jnp.sum(p, axis=-1, keepdims=True)
        a_refs[slot][...] = alpha
        p_refs[slot][...] = p.astype(BF16)

    def pv(t, slot):
        h, rows = split(t)
        acc_ref[h] = a_refs[slot][...] * acc_ref[h] + jnp.dot(
            p_refs[slot][...], v_ref[rows, :], preferred_element_type=F32)

    qk(0, 0)
    qk(1, 1)
    sm(0, 0)

    def body(i, carry):
        t = 2 * i + 1
        qk(t + 1, 0)
        sm(t, 1)
        pv(t - 1, 0)
        qk(t + 2, 1)
        sm(t + 1, 0)
        pv(t, 1)
        return carry

    lax.fori_loop(0, (steps - 2) // 2, body, 0)
    sm(steps - 1, 1)
    pv(steps - 2, 0)
    pv(steps - 1, 1)


def _gqa_kernel(q_ref, k_ref, v_ref, o_ref, lhs_ref, *scr, kv_chunk):
    tq = q_ref.shape[0]
    lane = lax.broadcasted_iota(jnp.int32, (tq, 128), 1)
    lo = lane < HEAD_DIM
    for j in range(GQA_KV_HEADS):
        lhs_ref[j, 0:tq, :] = q_ref[:, (2 * j) * 128:(2 * j + 1) * 128]
        lhs_ref[j, tq:2 * tq, :] = q_ref[:, (2 * j + 1) * 128:(2 * j + 2) * 128]
    _online_softmax(lhs_ref, k_ref, v_ref, kv_chunk, scr)
    l_ref, acc_ref = scr[1], scr[2]
    o0 = acc_ref[0] / l_ref[0]
    o1 = acc_ref[1] / l_ref[1]
    o_ref[:, 0:128] = jnp.where(lo, o0[0:tq], pltpu.roll(o0[tq:2 * tq], 64, 1))
    o_ref[:, 128:256] = jnp.where(lo, pltpu.roll(o1[0:tq], 64, 1), o1[tq:2 * tq])


def _diff_kernel(lam_ref, q_ref, k_ref, v_ref, ng_ref, seg_ref, o_ref, lhs_ref, *scr, kv_chunk,
                 out_scale):
    tq, gw = q_ref.shape
    lane = lax.broadcasted_iota(jnp.int32, (tq, gw), 1)
    q = q_ref[...].astype(F32)
    lam = lam_ref[0, 0]
    for h in range(GROUP_HEADS):
        base = h * HEAD_DIM
        q1 = jnp.where((lane >= base) & (lane < base + DIFF_QK_DIM), q, 0.0)
        q2 = jnp.where((lane >= base + DIFF_QK_DIM) & (lane < base + HEAD_DIM), q, 0.0)
        lhs_ref[h, 0:tq, :] = q1.astype(BF16)
        lhs_ref[h, tq:2 * tq, :] = q2.astype(BF16)
    _online_softmax(lhs_ref, k_ref, v_ref, kv_chunk, scr)
    l_ref, acc_ref = scr[1], scr[2]
    out = jnp.zeros((tq, gw), F32)
    for h in range(GROUP_HEADS):
        base = h * HEAD_DIM
        o = acc_ref[h] / l_ref[h]
        oh = o[0:tq] - lam * o[tq:2 * tq]
        out = jnp.where((lane >= base) & (lane < base + HEAD_DIM), oh, out)
    ms = jnp.dot(out * out, seg_ref[...], precision=HIGHEST, preferred_element_type=F32)
    o_ref[...] = out * lax.rsqrt(ms + NORM_EPS) * ng_ref[...] * out_scale


def _kv_chunk(s_len, n_heads):
    best = None
    for kc in range(128, min(s_len, ATT_KV_CHUNK_MAX) + 1, 128):
        if s_len % kc == 0 and n_heads * (s_len // kc) >= 4:
            best = kc
    assert best is not None, s_len
    return best


def _attn_call(kind, zq, kv, *, n_batch, t_len, ctx_len, ctx_queries, lam=None, norm_g=None,
               seg=None, out_scale=None):
    gw = GROUP_WIDTH
    lat_rows = n_batch * t_len
    s_all = ctx_len + t_len
    if kind == "gqa":
        qw, kw, n_heads = 512, 128, GQA_KV_HEADS
    else:
        qw, kw, n_heads = 256, 256, GROUP_HEADS
    if ctx_queries:
        tq, s_len = ctx_len, ctx_len
        grid = (n_batch, 1)
        q_map = lambda b, i: (lat_rows // ctx_len + b, 0)
        out_map = lambda b, i: (b, 0)
        kv_blk = lambda b: b * (s_all // ctx_len)
    else:
        tq, s_len = ATT_Q_TILE, s_all
        grid = (n_batch, t_len // tq)
        q_map = lambda b, i: (b * (t_len // tq) + i, 0)
        out_map = q_map
        kv_blk = lambda b: b
    kc = _kv_chunk(s_len, n_heads)
    kv_specs = [pl.BlockSpec((s_len, kw), lambda b, i: (kv_blk(b), 0)),
                pl.BlockSpec((s_len, kw), lambda b, i: (kv_blk(b), 1))]
    out_spec = pl.BlockSpec((tq, gw), out_map)
    out_shape = jax.ShapeDtypeStruct((n_batch * tq * grid[1], gw), F32)
    q_spec = pl.BlockSpec((tq, qw), q_map)
    const = lambda b, i: (0, 0)
    m_rows = 2 * tq
    scratch = [pltpu.VMEM((n_heads, m_rows, kw), BF16),
               pltpu.VMEM((n_heads, m_rows, 1), F32),
               pltpu.VMEM((n_heads, m_rows, 1), F32),
               pltpu.VMEM((n_heads, m_rows, kw), F32),
               pltpu.VMEM((m_rows, kc), F32), pltpu.VMEM((m_rows, kc), F32),
               pltpu.VMEM((m_rows, kc), BF16), pltpu.VMEM((m_rows, kc), BF16),
               pltpu.VMEM((m_rows, 1), F32), pltpu.VMEM((m_rows, 1), F32),
               pltpu.VMEM((m_rows, 1), F32), pltpu.VMEM((m_rows, 1), F32)]
    if kind == "gqa":
        return pl.pallas_call(
            functools.partial(_gqa_kernel, kv_chunk=kc),
            grid=grid, in_specs=[q_spec] + kv_specs, out_specs=out_spec, out_shape=out_shape,
            scratch_shapes=scratch,
            compiler_params=_cparams("arbitrary", "arbitrary"),
            name="gqa_ctx" if ctx_queries else "gqa_lat",
        )(zq, kv, kv)
    return pl.pallas_call(
        functools.partial(_diff_kernel, kv_chunk=kc, out_scale=out_scale),
        grid=grid,
        in_specs=[pl.BlockSpec(memory_space=pltpu.SMEM), q_spec] + kv_specs
        + [pl.BlockSpec((1, gw), const), pl.BlockSpec((gw, gw), const)],
        out_specs=out_spec, out_shape=out_shape,
        scratch_shapes=scratch,
        compiler_params=_cparams("arbitrary", "arbitrary"),
        name="diff_ctx" if ctx_queries else "diff_lat",
    )(lam, zq, kv, kv, norm_g, seg)


def _out_proj_kernel(rf_ref, rb_ref, rg_ref, ga_ref, hf_ref, hb_ref, hg_ref, df_ref, x_ref, mod_ref,
                     w_ref, rn_ref, hn_ref, seg_ref, n2_ref, rw_ref, rbias_ref,
                     xo_ref, h2_ref, e_ref, gt_ref):
    d = x_ref.shape[-1]
    gw = GROUP_WIDTH
    seg = seg_ref[...]
    mod = mod_ref[0]

    def readout(o, norm_g, gate):
        ms = jnp.dot(o * o, seg, precision=HIGHEST, preferred_element_type=F32)
        return o * lax.rsqrt(ms + NORM_EPS) * norm_g * (gate * _sigmoid(gate))

    parts = (readout(rf_ref[...] + rb_ref[...], rn_ref[...], rg_ref[...]),
             ga_ref[...],
             readout(hf_ref[...] + hb_ref[...], hn_ref[...], hg_ref[...]),
             df_ref[...])
    acc = jnp.zeros(x_ref.shape, F32)
    for n, part in enumerate(parts):
        acc = acc + jnp.dot(part.astype(BF16), w_ref[n * gw:(n + 1) * gw, :],
                            preferred_element_type=F32)
    x = x_ref[...] + mod[:, 2 * d:3 * d] * acc
    xo_ref[...] = x
    ms = jnp.mean(x * x, axis=-1, keepdims=True)
    h2 = x * lax.rsqrt(ms + NORM_EPS) * n2_ref[...]
    h2 = h2 * (1.0 + mod[:, 4 * d:5 * d]) + mod[:, 3 * d:4 * d]
    h2_ref[...] = h2

    logits = lax.dot_general(rw_ref[...], h2, NT_DIMS, precision=HIGHEST, preferred_element_type=F32)
    score = _sigmoid(logits)
    sel = score + rbias_ref[:, 0:1]
    ng = N_EXPERT_GROUPS
    sc = [score[m * ng:(m + 1) * ng, :] for m in range(EXPERTS_PER_GROUP)]
    sl = [sel[m * ng:(m + 1) * ng, :] for m in range(EXPERTS_PER_GROUP)]
    hi1, lo1 = jnp.maximum(sl[0], sl[1]), jnp.minimum(sl[0], sl[1])
    hi2, lo2 = jnp.maximum(sl[2], sl[3]), jnp.minimum(sl[2], sl[3])
    group_score = jnp.maximum(hi1, hi2) + jnp.maximum(jnp.minimum(hi1, hi2), jnp.maximum(lo1, lo2))
    g_iota = lax.broadcasted_iota(jnp.int32, group_score.shape, 0).astype(F32)
    g_max = jnp.max(group_score, axis=0, keepdims=True)
    best = jnp.min(jnp.where(group_score == g_max, g_iota, float(ng)), axis=0, keepdims=True)
    pick = g_iota == best
    v = [jnp.sum(jnp.where(pick, a, 0.0), axis=0, keepdims=True) for a in sl]
    u = [jnp.sum(jnp.where(pick, a, 0.0), axis=0, keepdims=True) for a in sc]
    rank = []
    for i in range(EXPERTS_PER_GROUP):
        r_i = jnp.zeros(best.shape, F32)
        for j in range(EXPERTS_PER_GROUP):
            if j < i:
                r_i = r_i + jnp.where(v[j] >= v[i], 1.0, 0.0)
            elif j > i:
                r_i = r_i + jnp.where(v[j] > v[i], 1.0, 0.0)
        rank.append(r_i)
    idx, gate = [], []
    for kth in range(2):
        idx.append(sum(jnp.where(rank[i] == kth, float(i), 0.0) for i in range(EXPERTS_PER_GROUP)))
        gate.append(sum(jnp.where(rank[i] == kth, u[i], 0.0) for i in range(EXPERTS_PER_GROUP)))
    tot = gate[0] + gate[1]
    e_ref[0:1, :] = (best * EXPERTS_PER_GROUP + idx[0]).astype(jnp.int32)
    e_ref[1:2, :] = (best * EXPERTS_PER_GROUP + idx[1]).astype(jnp.int32)
    gt_ref[0:1, :] = gate[0] / tot
    gt_ref[1:2, :] = gate[1] / tot


def _out_proj_call(rf, rb, zr, o_gqa, hf, hb, zh, o_df, x_all, mod_all, w_out_bf, rn, hn, seg, n2,
                   rw_t, rbias, n_rows, n_lat_rows, t_len):
    d = x_all.shape[-1]
    tm = ROW_TILE
    gw = GROUP_WIDTH
    n_lat_tiles = n_lat_rows // tm
    tiles_per_seq = t_len // tm
    n_batch = n_lat_rows // t_len

    def mod_idx(i):
        return (jnp.where(i < n_lat_tiles, i // tiles_per_seq, n_batch), 0, 0)

    row = lambda i: (i, 0)
    const = lambda i: (0, 0)
    blk = lambda col: pl.BlockSpec((tm, gw), lambda i: (i, col))
    return pl.pallas_call(
        _out_proj_kernel,
        grid=(n_rows // tm,),
        in_specs=[blk(0), blk(0), blk(3), blk(0), blk(0), blk(0), blk(4), blk(0),
                  pl.BlockSpec((tm, d), row),
                  pl.BlockSpec((1, 1, mod_all.shape[-1]), mod_idx),
                  pl.BlockSpec((4 * gw, d), const),
                  pl.BlockSpec((1, gw), const), pl.BlockSpec((1, gw), const),
                  pl.BlockSpec((gw, gw), const),
                  pl.BlockSpec((1, d), const),
                  pl.BlockSpec((N_EXPERTS, d), const),
                  pl.BlockSpec((N_EXPERTS, 128), const)],
        out_specs=[pl.BlockSpec((tm, d), row), pl.BlockSpec((tm, d), row),
                   pl.BlockSpec((2, tm), lambda i: (0, i)), pl.BlockSpec((2, tm), lambda i: (0, i))],
        out_shape=[jax.ShapeDtypeStruct((n_rows, d), F32), jax.ShapeDtypeStruct((n_rows, d), F32),
                   jax.ShapeDtypeStruct((2, n_rows), jnp.int32),
                   jax.ShapeDtypeStruct((2, n_rows), F32)],
        compiler_params=_cparams("arbitrary"),
        name="out_proj",
    )(rf, rb, zr, o_gqa, hf, hb, zh, o_df, x_all, mod_all, w_out_bf, rn, hn, seg, n2, rw_t, rbias)


def _expert_kernel(be_ref, nb_ref, idx_hbm, gate_ref, h2_hbm, wg_ref, wu_ref, wd_ref, ys_hbm,
                   idx_ref, x_ref, y_ref, wgb_ref, wub_ref, wdb_ref, sem_i, sem_g, sem_s):
    i = pl.program_id(0)
    nb = nb_ref[0]
    rows = MOE_BLOCK

    def idx_copy(blk):
        return pltpu.make_async_copy(idx_hbm.at[blk], idx_ref.at[blk % 3], sem_i.at[blk % 3])

    def issue_gather(blk):
        islot, xslot = blk % 3, blk % 2
        for r in range(rows):
            pltpu.make_async_copy(h2_hbm.at[pl.ds(idx_ref[islot, r], 1)],
                                  x_ref.at[xslot, pl.ds(r, 1)], sem_g.at[xslot]).start()

    def wait_slot(buf_ref, sem, slot):
        pltpu.make_async_copy(buf_ref.at[slot], buf_ref.at[slot], sem.at[slot]).wait()

    @pl.when(i < nb)
    def _():
        slot = i % 2

        @pl.when(i == 0)
        def _():
            y_ref[...] = jnp.zeros(y_ref.shape, F32)
            n_real = ys_hbm.shape[0] - 2 * rows
            for s in range(2):
                fill = pltpu.make_async_copy(y_ref.at[s], ys_hbm.at[pl.ds(n_real + s * rows, rows)],
                                             sem_s.at[s])
                fill.start()
                fill.wait()
            idx_copy(0).start()
            idx_copy(0).wait()
            issue_gather(0)

            @pl.when(nb > 1)
            def _():
                idx_copy(1).start()

        @pl.when(i + 1 < nb)
        def _():
            idx_copy(i + 1).wait()
            issue_gather(i + 1)

            @pl.when(i + 2 < nb)
            def _():
                idx_copy(i + 2).start()

        wait_slot(x_ref, sem_g, slot)

        @pl.when(i >= 2)
        def _():
            wait_slot(y_ref, sem_s, slot)

        @pl.when((i == 0) | (be_ref[i] != be_ref[jnp.maximum(i - 1, 0)]))
        def _():
            wgb_ref[...] = wg_ref[0].astype(BF16)
            wub_ref[...] = wu_ref[0].astype(BF16)
            wdb_ref[...] = wd_ref[0].astype(BF16)

        xb = x_ref[slot].astype(BF16)
        a = jnp.dot(xb, wgb_ref[...], preferred_element_type=F32)
        u = jnp.dot(xb, wub_ref[...], preferred_element_type=F32)
        hmid = (a * _sigmoid(a) * u).astype(BF16)
        y_ref[slot] = jnp.dot(hmid, wdb_ref[...], preferred_element_type=F32) * gate_ref[:, 0:1]

        islot = i % 3
        for r in range(rows):
            pltpu.make_async_copy(y_ref.at[slot, pl.ds(r, 1)],
                                  ys_hbm.at[pl.ds(idx_ref[islot, rows + r], 1)], sem_s.at[slot]).start()

        @pl.when(i == nb - 1)
        def _():
            @pl.when(i >= 1)
            def _():
                wait_slot(y_ref, sem_s, 1 - slot)
            wait_slot(y_ref, sem_s, slot)


def _expert_call(block_expert, n_used, row_idx, row_gate, h2, wg, wu, wd, n_tok):
    n_blocks = block_expert.shape[0]
    d = h2.shape[-1]
    ff = wg.shape[-1]
    rows = MOE_BLOCK
    grid_spec = pltpu.PrefetchScalarGridSpec(
        num_scalar_prefetch=2,
        grid=(n_blocks,),
        in_specs=[pl.BlockSpec(memory_space=pl.ANY),
                  pl.BlockSpec((rows, 1), lambda i, be, nb: (i, 0)),
                  pl.BlockSpec(memory_space=pl.ANY),
                  pl.BlockSpec((1, d, ff), lambda i, be, nb: (be[i], 0, 0)),
                  pl.BlockSpec((1, d, ff), lambda i, be, nb: (be[i], 0, 0)),
                  pl.BlockSpec((1, ff, d), lambda i, be, nb: (be[i], 0, 0))],
        out_specs=pl.BlockSpec(memory_space=pl.ANY),
        scratch_shapes=[pltpu.SMEM((3, 2 * rows), jnp.int32),
                        pltpu.VMEM((2, rows, d), F32),
                        pltpu.VMEM((2, rows, d), F32),
                        pltpu.VMEM((d, ff), BF16), pltpu.VMEM((d, ff), BF16), pltpu.VMEM((ff, d), BF16),
                        pltpu.SemaphoreType.DMA((3,)), pltpu.SemaphoreType.DMA((2,)),
                        pltpu.SemaphoreType.DMA((2,))],
    )
    return pl.pallas_call(
        _expert_kernel,
        grid_spec=grid_spec,
        out_shape=jax.ShapeDtypeStruct((2 * n_tok + 2 * rows, d), F32),
        compiler_params=_cparams("arbitrary"),
        name="experts",
    )(block_expert, n_used, row_idx, row_gate, h2, wg, wu, wd)


def _dispatch_plan(e_t, g_t, n_tok):
    m = 2 * n_tok
    e_flat = e_t.reshape(m)
    order = jnp.argsort(e_flat).astype(jnp.int32)
    experts = jnp.arange(N_EXPERTS, dtype=jnp.int32)
    counts = jnp.sum((e_flat[:, None] == experts[None, :]).astype(jnp.int32), axis=0)
    padded = (counts + MOE_BLOCK - 1) // MOE_BLOCK * MOE_BLOCK
    start = jnp.cumsum(counts) - counts
    pend = jnp.cumsum(padded)
    pstart = pend - padded
    n_blocks = -(-m // MOE_BLOCK) + N_EXPERTS
    blk_row0 = jnp.arange(n_blocks, dtype=jnp.int32) * MOE_BLOCK
    block_expert = jnp.minimum(
        jnp.sum((pend[None, :] <= blk_row0[:, None]).astype(jnp.int32), axis=1), N_EXPERTS - 1)
    blk = jnp.arange(n_blocks, dtype=jnp.int32)[:, None]
    r_in = jnp.arange(MOE_BLOCK, dtype=jnp.int32)[None, :]
    pos = blk_row0[:, None] + r_in - pstart[block_expert][:, None]
    valid = pos < counts[block_expert][:, None]
    sorted_pos = jnp.clip(start[block_expert][:, None] + pos, 0, m - 1)
    assign = order[sorted_pos]
    dump = m + (blk % 2) * MOE_BLOCK + r_in
    row_src = jnp.where(valid, assign % n_tok, 0)
    row_dst = jnp.where(valid, assign, dump)
    row_idx = jnp.concatenate([row_src, row_dst], axis=1).astype(jnp.int32)
    row_gate = jnp.where(valid, g_t.reshape(m)[assign], 0.0)
    n_used = (pend[-1] // MOE_BLOCK).astype(jnp.int32).reshape(1)
    return block_expert.astype(jnp.int32), n_used, row_idx, row_gate.reshape(n_blocks * MOE_BLOCK, 1)


def _combine_kernel(x_ref, y0_ref, y1_ref, mod_ref, fg_ref, o_ref, *, final_norm):
    d = x_ref.shape[-1]
    mod = mod_ref[0]
    x = x_ref[...] + mod[:, 5 * d:6 * d] * (y0_ref[...] + y1_ref[...])
    if final_norm:
        ms = jnp.mean(x * x, axis=-1, keepdims=True)
        x = x * lax.rsqrt(ms + NORM_EPS) * fg_ref[...]
    o_ref[...] = x


def _combine_call(x_new, ys, mod_all, final_g, n_rows, n_lat_rows, t_len, final_norm):
    d = x_new.shape[-1]
    tm = ROW_TILE
    n_lat_tiles = n_lat_rows // tm
    tiles_per_seq = t_len // tm
    n_batch = n_lat_rows // t_len
    n_tiles = n_rows // tm

    def mod_idx(i):
        return (jnp.where(i < n_lat_tiles, i // tiles_per_seq, n_batch), 0, 0)

    return pl.pallas_call(
        functools.partial(_combine_kernel, final_norm=final_norm),
        grid=(n_tiles,),
        in_specs=[pl.BlockSpec((tm, d), lambda i: (i, 0)),
                  pl.BlockSpec((tm, d), lambda i: (i, 0)),
                  pl.BlockSpec((tm, d), lambda i: (i + n_tiles, 0)),
                  pl.BlockSpec((1, 1, mod_all.shape[-1]), mod_idx),
                  pl.BlockSpec((1, d), lambda i: (0, 0))],
        out_specs=pl.BlockSpec((tm, d), lambda i: (i, 0)),
        out_shape=jax.ShapeDtypeStruct((n_rows, d), F32),
        compiler_params=_cparams("arbitrary"),
        name="combine",
    )(x_new, ys, ys, mod_all, final_g)


def _rope_tables(t_len, rot_dim, pad_rows):
    n_freq = rot_dim // 4
    half = rot_dim // 2
    inv_freq = ROPE_BASE ** (-jnp.arange(n_freq, dtype=F32) / n_freq)
    pos = jnp.arange(t_len)
    row = (pos // GRID_W).astype(F32)
    col = (pos % GRID_W).astype(F32)
    ang = jnp.concatenate([row[:, None] * inv_freq, col[:, None] * inv_freq], axis=-1)
    cos, sin = jnp.cos(ang), jnp.sin(ang)
    reps = GROUP_WIDTH // rot_dim
    zeros = jnp.zeros_like(sin)
    cos_f = jnp.tile(jnp.concatenate([cos, cos], -1), (1, reps))
    sin_a = jnp.tile(jnp.concatenate([-sin, zeros], -1), (1, reps))
    sin_b = jnp.tile(jnp.concatenate([zeros, sin], -1), (1, reps))
    pad1 = jnp.ones((pad_rows, GROUP_WIDTH), F32)
    pad0 = jnp.zeros((pad_rows, GROUP_WIDTH), F32)
    return (jnp.concatenate([cos_f, pad1], 0), jnp.concatenate([sin_a, pad0], 0),
            jnp.concatenate([sin_b, pad0], 0))


def kernel(x, c, ctx, c_ctx, w_mod, b_mod, norm1_g, norm2_g, w_in, ret_decay_logit, ret_norm_g,
           gqa_qnorm_g, gqa_knorm_g, hgrn_lb_logit, hgrn_norm_g, diff_lambda, diff_norm_g, w_out,
           router_w, router_bias, moe_w_gate, moe_w_up, moe_w_down, final_norm_g):
    n_batch, t_len, d = x.shape
    ctx_len = ctx.shape[1]
    depth = w_mod.shape[0]
    gw = GROUP_WIDTH
    lat_rows = n_batch * t_len
    all_rows = lat_rows + n_batch * ctx_len
    assert t_len % ROW_TILE == 0 and ctx_len % ROW_TILE == 0 and t_len % ATT_Q_TILE == 0
    assert ctx_len % REC_CHUNK == 0 and (ctx_len + t_len) % ctx_len == 0 and n_batch + 1 <= 8

    x_all = jnp.concatenate([x.reshape(lat_rows, d), ctx.reshape(n_batch * ctx_len, d)], axis=0)
    cvec = jnp.zeros((8, d), F32).at[:n_batch].set(c).at[n_batch].set(c_ctx)

    tabs = _rope_tables(t_len, HEAD_DIM, ROW_TILE) + _rope_tables(t_len, DIFF_QK_DIM, ROW_TILE)
    head_of = jnp.arange(gw) // HEAD_DIM
    same_head = head_of[:, None] == head_of[None, :]
    seg = same_head.astype(F32) / HEAD_DIM
    bd = same_head.astype(BF16)

    sm = jax.nn.softmax(hgrn_lb_logit.astype(F32), axis=1)
    lower_bounds = jnp.cumsum(sm, axis=1) - sm[:, :1]
    log_gamma = jax.nn.log_sigmoid(ret_decay_logit.astype(F32))

    perm = (jnp.arange(N_EXPERTS) % N_EXPERT_GROUPS) * EXPERTS_PER_GROUP + jnp.arange(N_EXPERTS) // N_EXPERT_GROUPS
    rw_t = router_w.T[perm]
    rbias = jnp.broadcast_to(router_bias.astype(F32)[perm][:, None], (N_EXPERTS, 128))

    tile = lambda g: jnp.tile(g.astype(F32), GROUP_HEADS).reshape(1, gw)

    out = None
    for layer in range(depth):
        need_ctx = layer < depth - 1
        lambda_init = 0.8 - 0.6 * math.exp(-0.3 * layer)
        mod_all = _mod_call(cvec, w_mod[layer], b_mod[layer]).reshape(8, 1, 6 * d)

        w_l = w_in[layer]
        w_bf = w_l.astype(BF16)
        wvt_bf = jnp.concatenate([w_l[:, 2 * gw:3 * gw], w_l[:, 9 * gw:10 * gw]], axis=1).T.astype(BF16)
        zr, zg, zh, zd, vt_all, kvg, kvd = _in_proj_call(
            x_all, mod_all, norm1_g[layer].reshape(1, d), w_bf, wvt_bf, tabs,
            tile(gqa_qnorm_g[layer]), tile(gqa_knorm_g[layer])[:, :128], seg, lat_rows, t_len)

        rec_kw = dict(n_batch=n_batch, t_len=t_len, ctx_len=ctx_len)
        ret_par = jnp.repeat(log_gamma[layer], HEAD_DIM, axis=-1)
        rf, rb = _rec_call(zr, vt_all, ret_par, bd, hgrn=False, cols=(0, 1, 1, 2), vt_row=0, **rec_kw)
        hf, hb = _rec_call(zh, vt_all, lower_bounds[:, layer], bd, hgrn=True, cols=(0, 1, 2, 3),
                           vt_row=1, **rec_kw)

        lp = diff_lambda[layer].astype(F32)
        lam = (jnp.exp(jnp.sum(lp[0] * lp[1])) - jnp.exp(jnp.sum(lp[2] * lp[3])) + lambda_init)
        lam = lam.reshape(1, 1)
        att_kw = dict(n_batch=n_batch, t_len=t_len, ctx_len=ctx_len)
        diff_kw = dict(lam=lam, norm_g=tile(diff_norm_g[layer]), seg=seg, out_scale=1.0 - lambda_init)
        o_gqa = _attn_call("gqa", zg, kvg, ctx_queries=False, **att_kw)
        o_df = _attn_call("diff", zd, kvd, ctx_queries=False, **att_kw, **diff_kw)
        n_rows = lat_rows
        if need_ctx:
            o_gqa = jnp.concatenate([o_gqa, _attn_call("gqa", zg, kvg, ctx_queries=True, **att_kw)], 0)
            o_df = jnp.concatenate(
                [o_df, _attn_call("diff", zd, kvd, ctx_queries=True, **att_kw, **diff_kw)], 0)
            n_rows = all_rows

        x_new, h2, e_t, g_t = _out_proj_call(
            rf, rb, zr, o_gqa, hf, hb, zh, o_df, x_all, mod_all, w_out[layer].astype(BF16),
            tile(ret_norm_g[layer]), tile(hgrn_norm_g[layer]), seg, norm2_g[layer].reshape(1, d),
            rw_t, rbias, n_rows, lat_rows, t_len)

        plan = _dispatch_plan(e_t, g_t, n_rows)
        ys = _expert_call(*plan, h2, moe_w_gate[layer], moe_w_up[layer],
                          moe_w_down[layer], n_rows)
        last = layer == depth - 1
        x_next = _combine_call(x_new, ys, mod_all, final_norm_g.reshape(1, d), n_rows, lat_rows, t_len,
                               final_norm=last)
        if last:
            out = x_next[:lat_rows].reshape(n_batch, t_len, d)
        else:
            x_all = x_next
    return out
```

```python
import functools
import math

import jax
import jax.numpy as jnp
from jax import lax
from jax.experimental import pallas as pl
from jax.experimental.pallas import tpu as pltpu

F32 = jnp.float32
BF16 = jnp.bfloat16
HIGHEST = lax.Precision.HIGHEST

HEAD_DIM = 64
GROUP_HEADS = 4
GROUP_WIDTH = GROUP_HEADS * HEAD_DIM
GQA_KV_HEADS = 2
DIFF_QK_DIM = HEAD_DIM // 2
GRID_W = 64
ROPE_BASE = 10000.0
N_EXPERTS = 32
N_EXPERT_GROUPS = 8
EXPERTS_PER_GROUP = 4
MOE_BLOCK = 256
NORM_EPS = 1e-6

ROW_TILE = 256
REC_CHUNK = 128
REC_SUB = 16
ATT_Q_TILE = 256
ATT_KV_CHUNK_MAX = 1536
VMEM_LIMIT = 56 * 1024 * 1024

NT_DIMS = (((1,), (1,)), ((), ()))
LOG2E = math.log2(math.e)


def _sigmoid(v):
    return 1.0 / (1.0 + jnp.exp(-v))


def _cparams(*sem):
    return pltpu.CompilerParams(dimension_semantics=sem, vmem_limit_bytes=VMEM_LIMIT)


def _mod_kernel(c_ref, w_ref, b_ref, o_ref):
    cv = c_ref[...]
    s = cv * _sigmoid(cv)
    o_ref[...] = jnp.dot(s, w_ref[...], precision=HIGHEST, preferred_element_type=F32) + b_ref[...]


def _mod_call(cvec, w_mod, b_mod):
    d, n = w_mod.shape
    tn = 1024
    return pl.pallas_call(
        _mod_kernel,
        grid=(n // tn,),
        in_specs=[pl.BlockSpec((8, d), lambda j: (0, 0)),
                  pl.BlockSpec((d, tn), lambda j: (0, j)),
                  pl.BlockSpec((1, tn), lambda j: (0, j))],
        out_specs=pl.BlockSpec((8, tn), lambda j: (0, j)),
        out_shape=jax.ShapeDtypeStruct((8, n), F32),
        compiler_params=_cparams("arbitrary"),
        name="mod",
    )(cvec, w_mod, b_mod.reshape(1, n))


def _rope(v, cos, sin_a, sin_b, half):
    n = v.shape[-1]
    return v * cos + pltpu.roll(v, n - half, 1) * sin_a + pltpu.roll(v, half, 1) * sin_b


def _in_proj_kernel(x_ref, mod_ref, g1_ref, w_ref, wvt_ref, c64_ref, sa64_ref, sb64_ref,
                    c32_ref, sa32_ref, sb32_ref, qg_ref, kg_ref, seg_ref,
                    zr_ref, zg_ref, zh_ref, zd_ref, vt_ref, gk_ref, gv_ref, kd_ref, vd_ref):
    d = x_ref.shape[-1]
    gw = GROUP_WIDTH
    x = x_ref[...]
    ms = jnp.mean(x * x, axis=-1, keepdims=True)
    h = x * lax.rsqrt(ms + NORM_EPS) * g1_ref[...]
    mod = mod_ref[0]
    h = h * (1.0 + mod[:, d:2 * d]) + mod[:, 0:d]
    hb = h.astype(BF16)

    def proj(c0, c1):
        return jnp.dot(hb, w_ref[:, c0:c1], preferred_element_type=F32)

    c64, sa64, sb64 = c64_ref[...], sa64_ref[...], sb64_ref[...]
    c32, sa32, sb32 = c32_ref[...], sa32_ref[...], sb32_ref[...]
    seg = seg_ref[...]

    zr_ref[:, 0:gw] = _rope(proj(0, gw) * HEAD_DIM ** -0.5, c64, sa64, sb64, 32)
    zr_ref[:, gw:2 * gw] = _rope(proj(gw, 2 * gw), c64, sa64, sb64, 32)
    zr_ref[:, 2 * gw:4 * gw] = proj(2 * gw, 4 * gw)

    o = 4 * gw
    q = proj(o, o + gw)
    qms = jnp.dot(q * q, seg, precision=HIGHEST, preferred_element_type=F32)
    q = q * lax.rsqrt(qms + NORM_EPS) * qg_ref[...]
    q = _rope(q, c64, sa64, sb64, 32) * (HEAD_DIM ** -0.5 * LOG2E)
    lane = lax.broadcasted_iota(jnp.int32, (q.shape[0], 128), 1)
    lo = lane < HEAD_DIM
    qa, qb = q[:, 0:128], q[:, 128:256]
    zg_ref[:, 0:128] = jnp.where(lo, qa, 0.0).astype(BF16)
    zg_ref[:, 128:256] = jnp.where(lo, pltpu.roll(qa, 64, 1), 0.0).astype(BF16)
    zg_ref[:, 256:384] = jnp.where(lo, 0.0, pltpu.roll(qb, 64, 1)).astype(BF16)
    zg_ref[:, 384:512] = jnp.where(lo, 0.0, qb).astype(BF16)
    k = proj(o + gw, o + gw + 128)
    kms = jnp.dot(k * k, seg[0:128, 0:128], precision=HIGHEST, preferred_element_type=F32)
    k = k * lax.rsqrt(kms + NORM_EPS) * kg_ref[...]
    gk_ref[...] = _rope(k, c64[:, 0:128], sa64[:, 0:128], sb64[:, 0:128], 32).astype(BF16)
    v = proj(o + gw + 128, o + gw + 256)
    gv_ref[0] = jnp.where(lo, v, 1.0).astype(BF16)
    gv_ref[1] = jnp.where(lo, 1.0, v).astype(BF16)

    o = 6 * gw
    zh_ref[...] = proj(o, o + 5 * gw)

    o = 11 * gw
    zd_ref[...] = (_rope(proj(o, o + gw), c32, sa32, sb32, 16) * (DIFF_QK_DIM ** -0.5 * LOG2E)).astype(BF16)
    kd_ref[...] = _rope(proj(o + gw, o + 2 * gw), c32, sa32, sb32, 16).astype(BF16)
    v = proj(o + 2 * gw, o + 3 * gw)
    va, vb = v[:, 0:128], v[:, 128:256]
    vd_ref[0] = jnp.where(lo, va, 1.0).astype(BF16)
    vd_ref[1] = jnp.where(lo, pltpu.roll(va, 64, 1), 1.0).astype(BF16)
    vd_ref[2] = jnp.where(lo, vb, 1.0).astype(BF16)
    vd_ref[3] = jnp.where(lo, pltpu.roll(vb, 64, 1), 1.0).astype(BF16)

    vt_ref[...] = lax.dot_general(wvt_ref[...], hb, NT_DIMS, preferred_element_type=F32).astype(BF16)


def _in_proj_call(x_all, mod_all, g1, w_bf, wvt_bf, tabs, qg, kg, seg, n_lat_rows, t_len):
    r, d = x_all.shape
    tm = ROW_TILE
    gw = GROUP_WIDTH
    n_lat_tiles = n_lat_rows // tm
    tiles_per_seq = t_len // tm
    n_batch = n_lat_rows // t_len

    def mod_idx(i):
        return (jnp.where(i < n_lat_tiles, i // tiles_per_seq, n_batch), 0, 0)

    def tab_idx(i):
        return (jnp.where(i < n_lat_tiles, i % tiles_per_seq, tiles_per_seq), 0)

    ctx_tiles = (r - n_lat_rows) // n_batch // tm
    seq_tiles = tiles_per_seq + ctx_tiles

    def kv_idx(i):
        ci = i - n_lat_tiles
        lat = (i // tiles_per_seq) * seq_tiles + ctx_tiles + i % tiles_per_seq
        ctx = (ci // ctx_tiles) * seq_tiles + ci % ctx_tiles
        return (jnp.where(i < n_lat_tiles, lat, ctx), 0)

    row = lambda i: (i, 0)
    const = lambda i: (0, 0)
    tab_spec = pl.BlockSpec((tm, gw), tab_idx)
    return pl.pallas_call(
        _in_proj_kernel,
        grid=(r // tm,),
        in_specs=[pl.BlockSpec((tm, d), row),
                  pl.BlockSpec((1, 1, mod_all.shape[-1]), mod_idx),
                  pl.BlockSpec((1, d), const),
                  pl.BlockSpec(w_bf.shape, const),
                  pl.BlockSpec(wvt_bf.shape, const),
                  tab_spec, tab_spec, tab_spec, tab_spec, tab_spec, tab_spec,
                  pl.BlockSpec((1, gw), const),
                  pl.BlockSpec((1, 128), const),
                  pl.BlockSpec((gw, gw), const)],
        out_specs=[pl.BlockSpec((tm, 4 * gw), row),
                   pl.BlockSpec((tm, 2 * gw), row),
                   pl.BlockSpec((tm, 5 * gw), row),
                   pl.BlockSpec((tm, gw), row),
                   pl.BlockSpec((2 * gw, tm), lambda i: (0, i)),
                   pl.BlockSpec((tm, 128), kv_idx),
                   pl.BlockSpec((GQA_KV_HEADS, tm, 128), lambda i: (0,) + kv_idx(i)),
                   pl.BlockSpec((tm, gw), kv_idx),
                   pl.BlockSpec((GROUP_HEADS, tm, 128), lambda i: (0,) + kv_idx(i))],
        out_shape=[jax.ShapeDtypeStruct((r, 4 * gw), F32),
                   jax.ShapeDtypeStruct((r, 2 * gw), BF16),
                   jax.ShapeDtypeStruct((r, 5 * gw), F32),
                   jax.ShapeDtypeStruct((r, gw), BF16),
                   jax.ShapeDtypeStruct((2 * gw, r), BF16),
                   jax.ShapeDtypeStruct((r, 128), BF16),
                   jax.ShapeDtypeStruct((GQA_KV_HEADS, r, 128), BF16),
                   jax.ShapeDtypeStruct((r, gw), BF16),
                   jax.ShapeDtypeStruct((GROUP_HEADS, r, 128), BF16)],
        compiler_params=_cparams("arbitrary"),
        name="in_proj",
    )(x_all, mod_all, g1, w_bf, wvt_bf, *tabs, qg, kg, seg)


def _rec_direction(q_ref, k_ref, v_ref, vt_ref, par, bd_ref, o_ref, s_ref,
                   lc_ref, qt_ref, kh_ref, kk_ref, tot_ref, a_ref, *, reverse):
    c, gw = q_ref.shape
    sub = REC_SUB
    n_sub = c // sub
    sig = _sigmoid(k_ref[...])
    logf = jnp.log(par + (1.0 - par) * sig)
    kk = (1.0 - par) * (1.0 - sig)
    r_i = lax.broadcasted_iota(jnp.int32, (c, c), 0)
    c_i = lax.broadcasted_iota(jnp.int32, (c, c), 1)
    shift = sub.bit_length() - 1
    same = jnp.right_shift(r_i, shift) == jnp.right_shift(c_i, shift)
    if reverse:
        incl = same & (c_i >= r_i)
        excl = same & (c_i < r_i)
    else:
        incl = same & (c_i <= r_i)
        excl = same & (c_i > r_i)
    lc = jnp.dot(jnp.where(incl, 1.0, 0.0), logf, precision=HIGHEST, preferred_element_type=F32)
    rr = jnp.dot(jnp.where(excl, 1.0, 0.0), logf, precision=HIGHEST, preferred_element_type=F32)
    lc_ref[...] = lc * LOG2E
    qt_ref[...] = (q_ref[...] * jnp.exp(lc)).astype(BF16)
    kh_ref[...] = kk * jnp.exp(rr)
    kk_ref[...] = kk
    tot_ref[...] = lc + rr
    bd = bd_ref[...]
    t_loc = lax.broadcasted_iota(jnp.int32, (sub, gw), 0)
    row_id = lax.broadcasted_iota(jnp.int32, (c, gw), 0)
    half = gw // 2
    head_shift = HEAD_DIM.bit_length() - 1
    same_head = (jnp.right_shift(lax.broadcasted_iota(jnp.int32, (half, half), 0), head_shift)
                 == jnp.right_shift(lax.broadcasted_iota(jnp.int32, (half, half), 1), head_shift))
    quads = (slice(0, half), slice(half, gw))

    for j in range(n_sub):
        a = (n_sub - 1 - j) if reverse else j
        base = a * sub
        rows = pl.ds(base, sub)
        s_q = [s_ref[qd, qd] for qd in quads]
        qt_a = qt_ref[rows, :]
        o_inter = jnp.concatenate(
            [lax.dot_general(qt_a[:, qd], s.astype(BF16), NT_DIMS, preferred_element_type=F32)
             for qd, s in zip(quads, s_q)], axis=1)
        lc_a = lc_ref[rows, :]
        q_a = q_ref[rows, :]
        k_a = kk_ref[rows, :]
        v_a = v_ref[rows, :]
        for s in range(sub):
            valid = (t_loc <= s) if reverse else (t_loc >= s)
            e = jnp.where(valid, jnp.exp2(lc_a - lc_a[s:s + 1, :]), 0.0)
            a_ref[s * sub:(s + 1) * sub, :] = (q_a * e * k_a[s:s + 1, :]).astype(BF16)
        b = jnp.dot(a_ref[...], bd, preferred_element_type=F32)
        o_intra = jnp.zeros((sub, gw), F32)
        for s in range(sub):
            o_intra = o_intra + b[s * sub:(s + 1) * sub, :] * v_a[s:s + 1, :]
        o_ref[rows, :] = o_inter + o_intra
        in_sub = (row_id >= base) & (row_id < base + sub)
        kh_m = jnp.where(in_sub, kh_ref[...], 0.0).astype(BF16)
        decay = jnp.exp(tot_ref[pl.ds(base, 1), :])
        for qd, s in zip(quads, s_q):
            u = jnp.dot(vt_ref[qd, :], kh_m[:, qd], preferred_element_type=F32)
            s_ref[qd, qd] = s * decay[:, qd] + jnp.where(same_head, u, 0.0)


def _rec_kernel(qf_ref, kf_ref, vf_ref, vtf_ref, qb_ref, kb_ref, vb_ref, vtb_ref, par_ref, bd_ref,
                of_ref, ob_ref, sf_ref, sb_ref, lc_ref, qt_ref, kh_ref, kk_ref, tot_ref, a_ref):
    @pl.when(pl.program_id(1) == 0)
    def _():
        sf_ref[...] = jnp.zeros_like(sf_ref)
        sb_ref[...] = jnp.zeros_like(sb_ref)

    tmp = (lc_ref, qt_ref, kh_ref, kk_ref, tot_ref, a_ref)
    _rec_direction(qf_ref, kf_ref, vf_ref, vtf_ref, par_ref[0:1, :], bd_ref, of_ref, sf_ref, *tmp,
                   reverse=False)
    _rec_direction(qb_ref, kb_ref, vb_ref, vtb_ref, par_ref[1:2, :], bd_ref, ob_ref, sb_ref, *tmp,
                   reverse=True)


def _ret_direction(q_ref, k_ref, v_ref, vt_ref, logg, o_ref, s_ref, *, reverse):
    c, gw = q_ref.shape
    q = q_ref[...]
    k = k_ref[...]
    t_row = lax.broadcasted_iota(jnp.int32, (c, gw), 0).astype(F32)
    if reverse:
        n_q, n_k = c - t_row, t_row
    else:
        n_q, n_k = t_row + 1.0, (c - 1.0) - t_row
    s_t = s_ref[...]
    qt = (q * jnp.exp(n_q * logg)).astype(BF16)
    o = lax.dot_general(qt, s_t.astype(BF16), NT_DIMS, preferred_element_type=F32)
    r_i = lax.broadcasted_iota(jnp.int32, (c, c), 0)
    c_i = lax.broadcasted_iota(jnp.int32, (c, c), 1)
    dist = (c_i - r_i) if reverse else (r_i - c_i)
    allowed = dist >= 0
    dist_f = dist.astype(F32)
    lane = lax.broadcasted_iota(jnp.int32, (c, gw), 1)
    kb = k.astype(BF16)
    vb = v_ref[...].astype(BF16)
    for h in range(GROUP_HEADS):
        in_h = (lane >= h * HEAD_DIM) & (lane < (h + 1) * HEAD_DIM)
        mask = jnp.where(allowed, jnp.exp(dist_f * logg[:, h * HEAD_DIM:h * HEAD_DIM + 1]), 0.0)
        qh = jnp.where(in_h, q, 0.0).astype(BF16)
        sc = lax.dot_general(qh, kb, NT_DIMS, preferred_element_type=F32) * mask
        oh = jnp.dot(sc.astype(BF16), vb, preferred_element_type=F32)
        o = o + jnp.where(in_h, oh, 0.0)
    o_ref[...] = o
    kh = (k * jnp.exp(n_k * logg)).astype(BF16)
    u = jnp.dot(vt_ref[...], kh, preferred_element_type=F32)
    head_shift = HEAD_DIM.bit_length() - 1
    same_head = (jnp.right_shift(lax.broadcasted_iota(jnp.int32, (gw, gw), 0), head_shift)
                 == jnp.right_shift(lax.broadcasted_iota(jnp.int32, (gw, gw), 1), head_shift))
    s_ref[...] = s_t * jnp.exp(float(c) * logg) + jnp.where(same_head, u, 0.0)


def _ret_kernel(qf_ref, kf_ref, vf_ref, vtf_ref, qb_ref, kb_ref, vb_ref, vtb_ref, par_ref, bd_ref,
                of_ref, ob_ref, sf_ref, sb_ref):
    @pl.when(pl.program_id(1) == 0)
    def _():
        sf_ref[...] = jnp.zeros_like(sf_ref)
        sb_ref[...] = jnp.zeros_like(sb_ref)

    _ret_direction(qf_ref, kf_ref, vf_ref, vtf_ref, par_ref[0:1, :], of_ref, sf_ref, reverse=False)
    _ret_direction(qb_ref, kb_ref, vb_ref, vtb_ref, par_ref[1:2, :], ob_ref, sb_ref, reverse=True)


def _rec_call(z, vt_all, par, bd, *, hgrn, n_batch, t_len, ctx_len, cols, vt_row):
    r = z.shape[0]
    c = REC_CHUNK
    gw = GROUP_WIDTH
    n_ctx = ctx_len // c
    n_lat = t_len // c
    lat_blocks = n_batch * n_lat

    def fwd_blk(b, i):
        return jnp.where(i < n_ctx, lat_blocks + b * n_ctx + i, b * n_lat + i - n_ctx)

    def bwd_blk(b, i):
        return jnp.where(i < n_ctx, lat_blocks + b * n_ctx + (n_ctx - 1 - i),
                         b * n_lat + (n_lat - 1 - (i - n_ctx)))

    def zspec(blk, col):
        return pl.BlockSpec((c, gw), lambda b, i: (blk(b, i), col))

    def vtspec(blk):
        return pl.BlockSpec((gw, c), lambda b, i: (vt_row, blk(b, i)))

    const = lambda b, i: (0, 0)
    cq, ckf, ckb, cv = cols
    state_scratch = [pltpu.VMEM((gw, gw), F32), pltpu.VMEM((gw, gw), F32)]
    if hgrn:
        body = _rec_kernel
        scratch = state_scratch + [
            pltpu.VMEM((c, gw), F32), pltpu.VMEM((c, gw), BF16), pltpu.VMEM((c, gw), F32),
            pltpu.VMEM((c, gw), F32), pltpu.VMEM((c, gw), F32),
            pltpu.VMEM((REC_SUB * REC_SUB, gw), BF16)]
    else:
        body = _ret_kernel
        scratch = state_scratch
    out_f, out_b = pl.pallas_call(
        body,
        grid=(n_batch, n_ctx + n_lat),
        in_specs=[zspec(fwd_blk, cq), zspec(fwd_blk, ckf), zspec(fwd_blk, cv), vtspec(fwd_blk),
                  zspec(bwd_blk, cq), zspec(bwd_blk, ckb), zspec(bwd_blk, cv), vtspec(bwd_blk),
                  pl.BlockSpec((2, gw), const),
                  pl.BlockSpec((gw, gw), const)],
        out_specs=[pl.BlockSpec((c, gw), lambda b, i: (fwd_blk(b, i), 0)),
                   pl.BlockSpec((c, gw), lambda b, i: (bwd_blk(b, i), 0))],
        out_shape=[jax.ShapeDtypeStruct((r, gw), F32), jax.ShapeDtypeStruct((r, gw), F32)],
        scratch_shapes=scratch,
        compiler_params=_cparams("arbitrary", "arbitrary"),
        name="hgrn_rec" if hgrn else "ret_rec",
    )(z, z, z, vt_all, z, z, z, vt_all, par, bd)
    return out_f, out_b


def _online_softmax(lhs_ref, k_ref, v_ref, kv_chunk, scr):
    m_ref, acc_ref = scr[0:2]
    s_refs, p_refs, a_refs, x_refs = scr[2:4], scr[4:6], scr[6:8], scr[8:10]
    n_heads = lhs_ref.shape[0]
    n = k_ref.shape[0] // kv_chunk
    steps = n_heads * n
    assert steps % 2 == 0 and steps >= 4
    m_ref[...] = jnp.full(m_ref.shape, -jnp.inf, F32)
    acc_ref[...] = jnp.zeros(acc_ref.shape, F32)

    def split(t):
        if isinstance(t, int):
            return t // n, pl.ds((t % n) * kv_chunk, kv_chunk)
        h = t // n
        return h, pl.ds(pl.multiple_of((t - h * n) * kv_chunk, kv_chunk), kv_chunk)

    def qk(t, slot):
        h, rows = split(t)
        s = lax.dot_general(lhs_ref[h], k_ref[rows, :], NT_DIMS, preferred_element_type=F32)
        s_refs[slot][...] = s
        x_refs[slot][...] = jnp.max(s, axis=-1, keepdims=True)

    def sm(t, slot):
        h, _ = split(t)
        s = s_refs[slot][...]
        m = m_ref[h]
        m_new = jnp.maximum(m, x_refs[slot][...])
        m_ref[h] = m_new
        a_refs[slot][...] = jnp.exp2(m - m_new)
        p_refs[slot][...] = jnp.exp2((s - m_new).astype(BF16))

    def pv(t, slot):
        h, rows = split(t)
        v_blk = v_ref[rows, :] if len(v_ref.shape) == 2 else v_ref[h, rows, :]
        acc_ref[h] = a_refs[slot][...] * acc_ref[h] + jnp.dot(
            p_refs[slot][...], v_blk, preferred_element_type=F32)

    qk(0, 0)
    qk(1, 1)
    sm(0, 0)

    def body(i, carry):
        t = 2 * i + 1
        qk(t + 1, 0)
        sm(t, 1)
        pv(t - 1, 0)
        qk(t + 2, 1)
        sm(t + 1, 0)
        pv(t, 1)
        return carry

    lax.fori_loop(0, (steps - 2) // 2, body, 0)
    sm(steps - 1, 1)
    pv(steps - 2, 0)
    pv(steps - 1, 1)


def _gqa_kernel(q_ref, k_ref, v_ref, o_ref, lhs_ref, *scr, kv_chunk):
    tq = q_ref.shape[0]
    lane = lax.broadcasted_iota(jnp.int32, (tq, 128), 1)
    lo = lane < HEAD_DIM
    for j in range(GQA_KV_HEADS):
        lhs_ref[j, 0:tq, :] = q_ref[:, (2 * j) * 128:(2 * j + 1) * 128]
        lhs_ref[j, tq:2 * tq, :] = q_ref[:, (2 * j + 1) * 128:(2 * j + 2) * 128]
    _online_softmax(lhs_ref, k_ref, v_ref, kv_chunk, scr)
    acc_ref = scr[1]
    lo2 = lax.broadcasted_iota(jnp.int32, (2 * tq, 128), 1) < HEAD_DIM
    acc0, acc1 = acc_ref[0], acc_ref[1]
    o0 = acc0 / jnp.where(lo2, pltpu.roll(acc0, 64, 1), 1.0)
    o1 = acc1 / jnp.where(lo2, 1.0, pltpu.roll(acc1, 64, 1))
    o_ref[:, 0:128] = jnp.where(lo, o0[0:tq], pltpu.roll(o0[tq:2 * tq], 64, 1))
    o_ref[:, 128:256] = jnp.where(lo, pltpu.roll(o1[0:tq], 64, 1), o1[tq:2 * tq])


def _diff_kernel(lam_ref, q_ref, k_ref, v_ref, ng_ref, seg_ref, o_ref, lhs_ref, *scr, kv_chunk,
                 out_scale):
    tq, gw = q_ref.shape
    lane = lax.broadcasted_iota(jnp.int32, (tq, gw), 1)
    q = q_ref[...].astype(F32)
    lam = lam_ref[0, 0]
    for h in range(GROUP_HEADS):
        base = h * HEAD_DIM
        q1 = jnp.where((lane >= base) & (lane < base + DIFF_QK_DIM), q, 0.0)
        q2 = jnp.where((lane >= base + DIFF_QK_DIM) & (lane < base + HEAD_DIM), q, 0.0)
        lhs_ref[h, 0:tq, :] = q1.astype(BF16)
        lhs_ref[h, tq:2 * tq, :] = q2.astype(BF16)
    _online_softmax(lhs_ref, k_ref, v_ref, kv_chunk, scr)
    acc_ref = scr[1]
    lo = lax.broadcasted_iota(jnp.int32, (tq, 128), 1) < HEAD_DIM
    lo2 = lax.broadcasted_iota(jnp.int32, (2 * tq, 128), 1) < HEAD_DIM
    heads = []
    for h in range(GROUP_HEADS):
        acc = acc_ref[h]
        o = acc / jnp.where(lo2, pltpu.roll(acc, 64, 1), 1.0)
        heads.append(o[0:tq] - lam * o[tq:2 * tq])
    out = jnp.concatenate([jnp.where(lo, heads[0], pltpu.roll(heads[1], 64, 1)),
                           jnp.where(lo, heads[2], pltpu.roll(heads[3], 64, 1))], axis=1)
    ms = jnp.dot(out * out, seg_ref[...], precision=HIGHEST, preferred_element_type=F32)
    o_ref[...] = out * lax.rsqrt(ms + NORM_EPS) * ng_ref[...] * out_scale


def _kv_chunk(s_len, n_heads):
    best = None
    for kc in range(128, min(s_len, ATT_KV_CHUNK_MAX) + 1, 128):
        if s_len % kc == 0 and n_heads * (s_len // kc) >= 4:
            best = kc
    assert best is not None, s_len
    return best


def _attn_call(kind, zq, kv, *, n_batch, t_len, ctx_len, ctx_queries, lam=None, norm_g=None,
               seg=None, out_scale=None):
    gw = GROUP_WIDTH
    lat_rows = n_batch * t_len
    s_all = ctx_len + t_len
    if kind == "gqa":
        qw, kw, n_heads = 512, 128, GQA_KV_HEADS
    else:
        qw, kw, n_heads = 256, 256, GROUP_HEADS
    if ctx_queries:
        tq, s_len = ctx_len, ctx_len
        grid = (n_batch, 1)
        q_map = lambda b, i: (lat_rows // ctx_len + b, 0)
        out_map = lambda b, i: (b, 0)
        kv_blk = lambda b: b * (s_all // ctx_len)
    else:
        tq, s_len = ATT_Q_TILE, s_all
        grid = (n_batch, t_len // tq)
        q_map = lambda b, i: (b * (t_len // tq) + i, 0)
        out_map = q_map
        kv_blk = lambda b: b
    kc = _kv_chunk(s_len, n_heads)
    nv = 128
    kv_specs = [pl.BlockSpec((s_len, kw), lambda b, i: (kv_blk(b), 0)),
                pl.BlockSpec((n_heads, s_len, nv), lambda b, i: (0, kv_blk(b), 0))]
    kv_args = kv
    out_spec = pl.BlockSpec((tq, gw), out_map)
    out_shape = jax.ShapeDtypeStruct((n_batch * tq * grid[1], gw), F32)
    q_spec = pl.BlockSpec((tq, qw), q_map)
    const = lambda b, i: (0, 0)
    m_rows = 2 * tq
    scratch = [pltpu.VMEM((n_heads, m_rows, kw), BF16),
               pltpu.VMEM((n_heads, m_rows, 1), F32),
               pltpu.VMEM((n_heads, m_rows, nv), F32),
               pltpu.VMEM((m_rows, kc), F32), pltpu.VMEM((m_rows, kc), F32),
               pltpu.VMEM((m_rows, kc), BF16), pltpu.VMEM((m_rows, kc), BF16),
               pltpu.VMEM((m_rows, 1), F32), pltpu.VMEM((m_rows, 1), F32),
               pltpu.VMEM((m_rows, 1), F32), pltpu.VMEM((m_rows, 1), F32)]
    if kind == "gqa":
        return pl.pallas_call(
            functools.partial(_gqa_kernel, kv_chunk=kc),
            grid=grid, in_specs=[q_spec] + kv_specs, out_specs=out_spec, out_shape=out_shape,
            scratch_shapes=scratch,
            compiler_params=_cparams("arbitrary", "arbitrary"),
            name="gqa_ctx" if ctx_queries else "gqa_lat",
        )(zq, *kv_args)
    return pl.pallas_call(
        functools.partial(_diff_kernel, kv_chunk=kc, out_scale=out_scale),
        grid=grid,
        in_specs=[pl.BlockSpec(memory_space=pltpu.SMEM), q_spec] + kv_specs
        + [pl.BlockSpec((1, gw), const), pl.BlockSpec((gw, gw), const)],
        out_specs=out_spec, out_shape=out_shape,
        scratch_shapes=scratch,
        compiler_params=_cparams("arbitrary", "arbitrary"),
        name="diff_ctx" if ctx_queries else "diff_lat",
    )(lam, zq, *kv_args, norm_g, seg)


def _out_proj_kernel(rf_ref, rb_ref, rg_ref, ga_ref, hf_ref, hb_ref, hg_ref, df_ref, x_ref, mod_ref,
                     w_ref, rn_ref, hn_ref, seg_ref, n2_ref, rw_ref, rbias_ref,
                     xo_ref, h2_ref, e_ref, gt_ref):
    d = x_ref.shape[-1]
    gw = GROUP_WIDTH
    seg = seg_ref[...]
    mod = mod_ref[0]

    def readout(o, norm_g, gate):
        ms = jnp.dot(o * o, seg, precision=HIGHEST, preferred_element_type=F32)
        return o * lax.rsqrt(ms + NORM_EPS) * norm_g * (gate * _sigmoid(gate))

    parts = (readout(rf_ref[...] + rb_ref[...], rn_ref[...], rg_ref[...]),
             ga_ref[...],
             readout(hf_ref[...] + hb_ref[...], hn_ref[...], hg_ref[...]),
             df_ref[...])
    acc = jnp.zeros(x_ref.shape, F32)
    for n, part in enumerate(parts):
        acc = acc + jnp.dot(part.astype(BF16), w_ref[n * gw:(n + 1) * gw, :],
                            preferred_element_type=F32)
    x = x_ref[...] + mod[:, 2 * d:3 * d] * acc
    xo_ref[...] = x
    ms = jnp.mean(x * x, axis=-1, keepdims=True)
    h2 = x * lax.rsqrt(ms + NORM_EPS) * n2_ref[...]
    h2 = h2 * (1.0 + mod[:, 4 * d:5 * d]) + mod[:, 3 * d:4 * d]
    h2_ref[...] = h2

    logits = lax.dot_general(rw_ref[...], h2, NT_DIMS, precision=HIGHEST, preferred_element_type=F32)
    score = _sigmoid(logits)
    sel = score + rbias_ref[:, 0:1]
    ng = N_EXPERT_GROUPS
    sc = [score[m * ng:(m + 1) * ng, :] for m in range(EXPERTS_PER_GROUP)]
    sl = [sel[m * ng:(m + 1) * ng, :] for m in range(EXPERTS_PER_GROUP)]
    hi1, lo1 = jnp.maximum(sl[0], sl[1]), jnp.minimum(sl[0], sl[1])
    hi2, lo2 = jnp.maximum(sl[2], sl[3]), jnp.minimum(sl[2], sl[3])
    group_score = jnp.maximum(hi1, hi2) + jnp.maximum(jnp.minimum(hi1, hi2), jnp.maximum(lo1, lo2))
    g_iota = lax.broadcasted_iota(jnp.int32, group_score.shape, 0).astype(F32)
    g_max = jnp.max(group_score, axis=0, keepdims=True)
    best = jnp.min(jnp.where(group_score == g_max, g_iota, float(ng)), axis=0, keepdims=True)
    pick = g_iota == best
    v = [jnp.sum(jnp.where(pick, a, 0.0), axis=0, keepdims=True) for a in sl]
    u = [jnp.sum(jnp.where(pick, a, 0.0), axis=0, keepdims=True) for a in sc]
    rank = []
    for i in range(EXPERTS_PER_GROUP):
        r_i = jnp.zeros(best.shape, F32)
        for j in range(EXPERTS_PER_GROUP):
            if j < i:
                r_i = r_i + jnp.where(v[j] >= v[i], 1.0, 0.0)
            elif j > i:
                r_i = r_i + jnp.where(v[j] > v[i], 1.0, 0.0)
        rank.append(r_i)
    idx, gate = [], []
    for kth in range(2):
        idx.append(sum(jnp.where(rank[i] == kth, float(i), 0.0) for i in range(EXPERTS_PER_GROUP)))
        gate.append(sum(jnp.where(rank[i] == kth, u[i], 0.0) for i in range(EXPERTS_PER_GROUP)))
    tot = gate[0] + gate[1]
    e_ref[0:1, :] = (best * EXPERTS_PER_GROUP + idx[0]).astype(jnp.int32)
    e_ref[1:2, :] = (best * EXPERTS_PER_GROUP + idx[1]).astype(jnp.int32)
    gt_ref[0:1, :] = gate[0] / tot
    gt_ref[1:2, :] = gate[1] / tot


def _out_proj_call(rf, rb, zr, o_gqa, hf, hb, zh, o_df, x_all, mod_all, w_out_bf, rn, hn, seg, n2,
                   rw_t, rbias, n_rows, n_lat_rows, t_len):
    d = x_all.shape[-1]
    tm = ROW_TILE
    gw = GROUP_WIDTH
    n_lat_tiles = n_lat_rows // tm
    tiles_per_seq = t_len // tm
    n_batch = n_lat_rows // t_len

    def mod_idx(i):
        return (jnp.where(i < n_lat_tiles, i // tiles_per_seq, n_batch), 0, 0)

    row = lambda i: (i, 0)
    const = lambda i: (0, 0)
    blk = lambda col: pl.BlockSpec((tm, gw), lambda i: (i, col))
    return pl.pallas_call(
        _out_proj_kernel,
        grid=(n_rows // tm,),
        in_specs=[blk(0), blk(0), blk(3), blk(0), blk(0), blk(0), blk(4), blk(0),
                  pl.BlockSpec((tm, d), row),
                  pl.BlockSpec((1, 1, mod_all.shape[-1]), mod_idx),
                  pl.BlockSpec((4 * gw, d), const),
                  pl.BlockSpec((1, gw), const), pl.BlockSpec((1, gw), const),
                  pl.BlockSpec((gw, gw), const),
                  pl.BlockSpec((1, d), const),
                  pl.BlockSpec((N_EXPERTS, d), const),
                  pl.BlockSpec((N_EXPERTS, 128), const)],
        out_specs=[pl.BlockSpec((tm, d), row), pl.BlockSpec((tm, d), row),
                   pl.BlockSpec((2, tm), lambda i: (0, i)), pl.BlockSpec((2, tm), lambda i: (0, i))],
        out_shape=[jax.ShapeDtypeStruct((n_rows, d), F32), jax.ShapeDtypeStruct((n_rows, d), F32),
                   jax.ShapeDtypeStruct((2, n_rows), jnp.int32),
                   jax.ShapeDtypeStruct((2, n_rows), F32)],
        compiler_params=_cparams("arbitrary"),
        name="out_proj",
    )(rf, rb, zr, o_gqa, hf, hb, zh, o_df, x_all, mod_all, w_out_bf, rn, hn, seg, n2, rw_t, rbias)


def _expert_kernel(be_ref, nb_ref, idx_hbm, gate_ref, h2_hbm, wg_ref, wu_ref, wd_ref, ys_hbm,
                   idx_ref, x_ref, y_ref, wgb_ref, wub_ref, wdb_ref, sem_i, sem_g, sem_s):
    i = pl.program_id(0)
    nb = nb_ref[0]
    rows = MOE_BLOCK

    def idx_copy(blk):
        return pltpu.make_async_copy(idx_hbm.at[blk], idx_ref.at[blk % 3], sem_i.at[blk % 3])

    def issue_gather(blk):
        islot, xslot = blk % 3, blk % 2
        for r in range(rows):
            pltpu.make_async_copy(h2_hbm.at[pl.ds(idx_ref[islot, r], 1)],
                                  x_ref.at[xslot, pl.ds(r, 1)], sem_g.at[xslot]).start()

    def wait_slot(buf_ref, sem, slot):
        pltpu.make_async_copy(buf_ref.at[slot], buf_ref.at[slot], sem.at[slot]).wait()

    @pl.when(i < nb)
    def _():
        slot = i % 2

        @pl.when(i == 0)
        def _():
            y_ref[...] = jnp.zeros(y_ref.shape, F32)
            n_real = ys_hbm.shape[0] - 2 * rows
            for s in range(2):
                fill = pltpu.make_async_copy(y_ref.at[s], ys_hbm.at[pl.ds(n_real + s * rows, rows)],
                                             sem_s.at[s])
                fill.start()
                fill.wait()
            idx_copy(0).start()
            idx_copy(0).wait()
            issue_gather(0)

            @pl.when(nb > 1)
            def _():
                idx_copy(1).start()

        @pl.when(i + 1 < nb)
        def _():
            idx_copy(i + 1).wait()
            issue_gather(i + 1)

            @pl.when(i + 2 < nb)
            def _():
                idx_copy(i + 2).start()

        wait_slot(x_ref, sem_g, slot)

        @pl.when(i >= 2)
        def _():
            wait_slot(y_ref, sem_s, slot)

        @pl.when((i == 0) | (be_ref[i] != be_ref[jnp.maximum(i - 1, 0)]))
        def _():
            wgb_ref[...] = wg_ref[0].astype(BF16)
            wub_ref[...] = wu_ref[0].astype(BF16)
            wdb_ref[...] = wd_ref[0].astype(BF16)

        xb = x_ref[slot].astype(BF16)
        a = jnp.dot(xb, wgb_ref[...], preferred_element_type=F32)
        u = jnp.dot(xb, wub_ref[...], preferred_element_type=F32)
        hmid = (a * _sigmoid(a) * u).astype(BF16)
        y_ref[slot] = jnp.dot(hmid, wdb_ref[...], preferred_element_type=F32) * gate_ref[:, 0:1]

        islot = i % 3
        for r in range(rows):
            pltpu.make_async_copy(y_ref.at[slot, pl.ds(r, 1)],
                                  ys_hbm.at[pl.ds(idx_ref[islot, rows + r], 1)], sem_s.at[slot]).start()

        @pl.when(i == nb - 1)
        def _():
            @pl.when(i >= 1)
            def _():
                wait_slot(y_ref, sem_s, 1 - slot)
            wait_slot(y_ref, sem_s, slot)


def _expert_call(block_expert, n_used, row_idx, row_gate, h2, wg, wu, wd, n_tok, layer):
    n_blocks = block_expert.shape[0]
    d = h2.shape[-1]
    ff = wg.shape[-1]
    rows = MOE_BLOCK
    grid_spec = pltpu.PrefetchScalarGridSpec(
        num_scalar_prefetch=2,
        grid=(n_blocks,),
        in_specs=[pl.BlockSpec(memory_space=pl.ANY),
                  pl.BlockSpec((rows, 1), lambda i, be, nb: (i, 0)),
                  pl.BlockSpec(memory_space=pl.ANY),
                  pl.BlockSpec((None, 1, d, ff), lambda i, be, nb: (layer, be[i], 0, 0)),
                  pl.BlockSpec((None, 1, d, ff), lambda i, be, nb: (layer, be[i], 0, 0)),
                  pl.BlockSpec((None, 1, ff, d), lambda i, be, nb: (layer, be[i], 0, 0))],
        out_specs=pl.BlockSpec(memory_space=pl.ANY),
        scratch_shapes=[pltpu.SMEM((3, 2 * rows), jnp.int32),
                        pltpu.VMEM((2, rows, d), F32),
                        pltpu.VMEM((2, rows, d), F32),
                        pltpu.VMEM((d, ff), BF16), pltpu.VMEM((d, ff), BF16), pltpu.VMEM((ff, d), BF16),
                        pltpu.SemaphoreType.DMA((3,)), pltpu.SemaphoreType.DMA((2,)),
                        pltpu.SemaphoreType.DMA((2,))],
    )
    return pl.pallas_call(
        _expert_kernel,
        grid_spec=grid_spec,
        out_shape=jax.ShapeDtypeStruct((2 * n_tok + 2 * rows, d), F32),
        compiler_params=_cparams("arbitrary"),
        name="experts",
    )(block_expert, n_used, row_idx, row_gate, h2, wg, wu, wd)


def _dispatch_plan(e_t, g_t, n_tok):
    m = 2 * n_tok
    e_flat = e_t.reshape(m)
    order = jnp.argsort(e_flat).astype(jnp.int32)
    experts = jnp.arange(N_EXPERTS, dtype=jnp.int32)
    counts = jnp.sum((e_flat[:, None] == experts[None, :]).astype(jnp.int32), axis=0)
    padded = (counts + MOE_BLOCK - 1) // MOE_BLOCK * MOE_BLOCK
    start = jnp.cumsum(counts) - counts
    pend = jnp.cumsum(padded)
    pstart = pend - padded
    n_blocks = -(-m // MOE_BLOCK) + N_EXPERTS
    blk_row0 = jnp.arange(n_blocks, dtype=jnp.int32) * MOE_BLOCK
    block_expert = jnp.minimum(
        jnp.sum((pend[None, :] <= blk_row0[:, None]).astype(jnp.int32), axis=1), N_EXPERTS - 1)
    blk = jnp.arange(n_blocks, dtype=jnp.int32)[:, None]
    r_in = jnp.arange(MOE_BLOCK, dtype=jnp.int32)[None, :]
    pos = blk_row0[:, None] + r_in - pstart[block_expert][:, None]
    valid = pos < counts[block_expert][:, None]
    sorted_pos = jnp.clip(start[block_expert][:, None] + pos, 0, m - 1)
    assign = order[sorted_pos]
    dump = m + (blk % 2) * MOE_BLOCK + r_in
    row_src = jnp.where(valid, assign % n_tok, 0)
    row_dst = jnp.where(valid, assign, dump)
    row_idx = jnp.concatenate([row_src, row_dst], axis=1).astype(jnp.int32)
    row_gate = jnp.where(valid, g_t.reshape(m)[assign], 0.0)
    n_used = (pend[-1] // MOE_BLOCK).astype(jnp.int32).reshape(1)
    return block_expert.astype(jnp.int32), n_used, row_idx, row_gate.reshape(n_blocks * MOE_BLOCK, 1)


def _combine_kernel(x_ref, y0_ref, y1_ref, mod_ref, fg_ref, o_ref, *, final_norm):
    d = x_ref.shape[-1]
    mod = mod_ref[0]
    x = x_ref[...] + mod[:, 5 * d:6 * d] * (y0_ref[...] + y1_ref[...])
    if final_norm:
        ms = jnp.mean(x * x, axis=-1, keepdims=True)
        x = x * lax.rsqrt(ms + NORM_EPS) * fg_ref[...]
    o_ref[...] = x


def _combine_call(x_new, ys, mod_all, final_g, n_rows, n_lat_rows, t_len, final_norm):
    d = x_new.shape[-1]
    tm = ROW_TILE
    n_lat_tiles = n_lat_rows // tm
    tiles_per_seq = t_len // tm
    n_batch = n_lat_rows // t_len
    n_tiles = n_rows // tm

    def mod_idx(i):
        return (jnp.where(i < n_lat_tiles, i // tiles_per_seq, n_batch), 0, 0)

    return pl.pallas_call(
        functools.partial(_combine_kernel, final_norm=final_norm),
        grid=(n_tiles,),
        in_specs=[pl.BlockSpec((tm, d), lambda i: (i, 0)),
                  pl.BlockSpec((tm, d), lambda i: (i, 0)),
                  pl.BlockSpec((tm, d), lambda i: (i + n_tiles, 0)),
                  pl.BlockSpec((1, 1, mod_all.shape[-1]), mod_idx),
                  pl.BlockSpec((1, d), lambda i: (0, 0))],
        out_specs=pl.BlockSpec((tm, d), lambda i: (i, 0)),
        out_shape=jax.ShapeDtypeStruct((n_rows, d), F32),
        compiler_params=_cparams("arbitrary"),
        name="combine",
    )(x_new, ys, ys, mod_all, final_g)


def _rope_tables(t_len, rot_dim, pad_rows):
    n_freq = rot_dim // 4
    half = rot_dim // 2
    inv_freq = ROPE_BASE ** (-jnp.arange(n_freq, dtype=F32) / n_freq)
    pos = jnp.arange(t_len)
    row = (pos // GRID_W).astype(F32)
    col = (pos % GRID_W).astype(F32)
    ang = jnp.concatenate([row[:, None] * inv_freq, col[:, None] * inv_freq], axis=-1)
    cos, sin = jnp.cos(ang), jnp.sin(ang)
    reps = GROUP_WIDTH // rot_dim
    zeros = jnp.zeros_like(sin)
    cos_f = jnp.tile(jnp.concatenate([cos, cos], -1), (1, reps))
    sin_a = jnp.tile(jnp.concatenate([-sin, zeros], -1), (1, reps))
    sin_b = jnp.tile(jnp.concatenate([zeros, sin], -1), (1, reps))
    pad1 = jnp.ones((pad_rows, GROUP_WIDTH), F32)
    pad0 = jnp.zeros((pad_rows, GROUP_WIDTH), F32)
    return (jnp.concatenate([cos_f, pad1], 0), jnp.concatenate([sin_a, pad0], 0),
            jnp.concatenate([sin_b, pad0], 0))


def kernel(x, c, ctx, c_ctx, w_mod, b_mod, norm1_g, norm2_g, w_in, ret_decay_logit, ret_norm_g,
           gqa_qnorm_g, gqa_knorm_g, hgrn_lb_logit, hgrn_norm_g, diff_lambda, diff_norm_g, w_out,
           router_w, router_bias, moe_w_gate, moe_w_up, moe_w_down, final_norm_g):
    n_batch, t_len, d = x.shape
    ctx_len = ctx.shape[1]
    depth = w_mod.shape[0]
    gw = GROUP_WIDTH
    lat_rows = n_batch * t_len
    all_rows = lat_rows + n_batch * ctx_len
    assert t_len % ROW_TILE == 0 and ctx_len % ROW_TILE == 0 and t_len % ATT_Q_TILE == 0
    assert ctx_len % REC_CHUNK == 0 and (ctx_len + t_len) % ctx_len == 0 and n_batch + 1 <= 8

    x_all = jnp.concatenate([x.reshape(lat_rows, d), ctx.reshape(n_batch * ctx_len, d)], axis=0)
    cvec = jnp.zeros((8, d), F32).at[:n_batch].set(c).at[n_batch].set(c_ctx)

    tabs = _rope_tables(t_len, HEAD_DIM, ROW_TILE) + _rope_tables(t_len, DIFF_QK_DIM, ROW_TILE)
    head_of = jnp.arange(gw) // HEAD_DIM
    same_head = head_of[:, None] == head_of[None, :]
    seg = same_head.astype(F32) / HEAD_DIM
    bd = same_head.astype(BF16)

    sm = jax.nn.softmax(hgrn_lb_logit.astype(F32), axis=1)
    lower_bounds = jnp.cumsum(sm, axis=1) - sm[:, :1]
    log_gamma = jax.nn.log_sigmoid(ret_decay_logit.astype(F32))

    perm = (jnp.arange(N_EXPERTS) % N_EXPERT_GROUPS) * EXPERTS_PER_GROUP + jnp.arange(N_EXPERTS) // N_EXPERT_GROUPS
    rw_t = router_w.T[perm]
    rbias = jnp.broadcast_to(router_bias.astype(F32)[perm][:, None], (N_EXPERTS, 128))

    tile = lambda g: jnp.tile(g.astype(F32), GROUP_HEADS).reshape(1, gw)

    out = None
    for layer in range(depth):
        need_ctx = layer < depth - 1
        lambda_init = 0.8 - 0.6 * math.exp(-0.3 * layer)
        mod_all = _mod_call(cvec, w_mod[layer], b_mod[layer]).reshape(8, 1, 6 * d)

        w_l = w_in[layer]
        w_bf = w_l.astype(BF16)
        wvt_bf = jnp.concatenate([w_l[:, 2 * gw:3 * gw], w_l[:, 9 * gw:10 * gw]], axis=1).T.astype(BF16)
        zr, zg, zh, zd, vt_all, kg, vg, kd, vd = _in_proj_call(
            x_all, mod_all, norm1_g[layer].reshape(1, d), w_bf, wvt_bf, tabs,
            tile(gqa_qnorm_g[layer]), tile(gqa_knorm_g[layer])[:, :128], seg, lat_rows, t_len)

        rec_kw = dict(n_batch=n_batch, t_len=t_len, ctx_len=ctx_len)
        ret_par = jnp.repeat(log_gamma[layer], HEAD_DIM, axis=-1)
        rf, rb = _rec_call(zr, vt_all, ret_par, bd, hgrn=False, cols=(0, 1, 1, 2), vt_row=0, **rec_kw)
        hf, hb = _rec_call(zh, vt_all, lower_bounds[:, layer], bd, hgrn=True, cols=(0, 1, 2, 3),
                           vt_row=1, **rec_kw)

        lp = diff_lambda[layer].astype(F32)
        lam = (jnp.exp(jnp.sum(lp[0] * lp[1])) - jnp.exp(jnp.sum(lp[2] * lp[3])) + lambda_init)
        lam = lam.reshape(1, 1)
        att_kw = dict(n_batch=n_batch, t_len=t_len, ctx_len=ctx_len)
        diff_kw = dict(lam=lam, norm_g=tile(diff_norm_g[layer]), seg=seg, out_scale=1.0 - lambda_init)
        o_gqa = _attn_call("gqa", zg, (kg, vg), ctx_queries=False, **att_kw)
        o_df = _attn_call("diff", zd, (kd, vd), ctx_queries=False, **att_kw, **diff_kw)
        n_rows = lat_rows
        if need_ctx:
            o_gqa = jnp.concatenate(
                [o_gqa, _attn_call("gqa", zg, (kg, vg), ctx_queries=True, **att_kw)], 0)
            o_df = jnp.concatenate(
                [o_df, _attn_call("diff", zd, (kd, vd), ctx_queries=True, **att_kw, **diff_kw)], 0)
            n_rows = all_rows

        x_new, h2, e_t, g_t = _out_proj_call(
            rf, rb, zr, o_gqa, hf, hb, zh, o_df, x_all, mod_all, w_out[layer].astype(BF16),
            tile(ret_norm_g[layer]), tile(hgrn_norm_g[layer]), seg, norm2_g[layer].reshape(1, d),
            rw_t, rbias, n_rows, lat_rows, t_len)

        plan = _dispatch_plan(e_t, g_t, n_rows)
        ys = _expert_call(*plan, h2, moe_w_gate, moe_w_up, moe_w_down, n_rows, layer)
        last = layer == depth - 1
        x_next = _combine_call(x_new, ys, mod_all, final_norm_g.reshape(1, d), n_rows, lat_rows, t_len,
                               final_norm=last)
        if last:
            out = x_next[:lat_rows].reshape(n_batch, t_len, d)
        else:
            x_all = x_next
    return out
```

```python
import functools
import math

import jax
import jax.numpy as jnp
from jax import lax
from jax.experimental import pallas as pl
from jax.experimental.pallas import tpu as pltpu

F32 = jnp.float32
BF16 = jnp.bfloat16
HIGHEST = lax.Precision.HIGHEST

HEAD_DIM = 64
GROUP_HEADS = 4
GROUP_WIDTH = GROUP_HEADS * HEAD_DIM
GQA_KV_HEADS = 2
DIFF_QK_DIM = HEAD_DIM // 2
GRID_W = 64
ROPE_BASE = 10000.0
N_EXPERTS = 32
N_EXPERT_GROUPS = 8
EXPERTS_PER_GROUP = 4
MOE_BLOCK = 256
NORM_EPS = 1e-6

ROW_TILE = 256
OUT_TILE = 512
OUT_SUB_TILE = 128
REC_CHUNK = 128
REC_SUB = 16
ATT_Q_TILE = 256
ATT_KV_CHUNK_MAX = 1536
VMEM_LIMIT = 56 * 1024 * 1024

NT_DIMS = (((1,), (1,)), ((), ()))
LOG2E = math.log2(math.e)


def _sigmoid(v):
    return 1.0 / (1.0 + jnp.exp(-v))


def _cparams(*sem):
    return pltpu.CompilerParams(dimension_semantics=sem, vmem_limit_bytes=VMEM_LIMIT)


def _mod_kernel(c_ref, w_ref, b_ref, o_ref):
    cv = c_ref[...]
    s = cv * _sigmoid(cv)
    o_ref[...] = jnp.dot(s, w_ref[...], precision=HIGHEST, preferred_element_type=F32) + b_ref[...]


def _mod_call(cvec, w_mod, b_mod):
    d, n = w_mod.shape
    tn = 1024
    return pl.pallas_call(
        _mod_kernel,
        grid=(n // tn,),
        in_specs=[pl.BlockSpec((8, d), lambda j: (0, 0)),
                  pl.BlockSpec((d, tn), lambda j: (0, j)),
                  pl.BlockSpec((1, tn), lambda j: (0, j))],
        out_specs=pl.BlockSpec((8, tn), lambda j: (0, j)),
        out_shape=jax.ShapeDtypeStruct((8, n), F32),
        compiler_params=_cparams("arbitrary"),
        name="mod",
    )(cvec, w_mod, b_mod.reshape(1, n))


def _rope(v, cos, sin_a, sin_b, half):
    n = v.shape[-1]
    return v * cos + pltpu.roll(v, n - half, 1) * sin_a + pltpu.roll(v, half, 1) * sin_b


def _in_proj_kernel(x_ref, mod_ref, g1_ref, w_ref, wvt_ref, c64_ref, sa64_ref, sb64_ref,
                    c32_ref, sa32_ref, sb32_ref, qg_ref, kg_ref, seg_ref,
                    zr_ref, zg_ref, zh_ref, zd_ref, vt_ref, gk_ref, gv_ref, kd_ref, vd_ref):
    d = x_ref.shape[-1]
    gw = GROUP_WIDTH
    x = x_ref[...]
    ms = jnp.mean(x * x, axis=-1, keepdims=True)
    h = x * lax.rsqrt(ms + NORM_EPS) * g1_ref[...]
    mod = mod_ref[0]
    h = h * (1.0 + mod[:, d:2 * d]) + mod[:, 0:d]
    hb = h.astype(BF16)

    def proj(c0, c1):
        return jnp.dot(hb, w_ref[:, c0:c1], preferred_element_type=F32)

    c64, sa64, sb64 = c64_ref[...], sa64_ref[...], sb64_ref[...]
    c32, sa32, sb32 = c32_ref[...], sa32_ref[...], sb32_ref[...]
    seg = seg_ref[...]

    zr_ref[:, 0:gw] = _rope(proj(0, gw) * HEAD_DIM ** -0.5, c64, sa64, sb64, 32)
    zr_ref[:, gw:2 * gw] = _rope(proj(gw, 2 * gw), c64, sa64, sb64, 32)
    zr_ref[:, 2 * gw:4 * gw] = proj(2 * gw, 4 * gw)

    o = 4 * gw
    q = proj(o, o + gw)
    qms = jnp.dot(q * q, seg, precision=HIGHEST, preferred_element_type=F32)
    q = q * lax.rsqrt(qms + NORM_EPS) * qg_ref[...]
    q = _rope(q, c64, sa64, sb64, 32) * (HEAD_DIM ** -0.5 * LOG2E)
    lane = lax.broadcasted_iota(jnp.int32, (q.shape[0], 128), 1)
    lo = lane < HEAD_DIM
    qa, qb = q[:, 0:128], q[:, 128:256]
    zg_ref[:, 0:128] = jnp.where(lo, qa, 0.0).astype(BF16)
    zg_ref[:, 128:256] = jnp.where(lo, pltpu.roll(qa, 64, 1), 0.0).astype(BF16)
    zg_ref[:, 256:384] = jnp.where(lo, 0.0, pltpu.roll(qb, 64, 1)).astype(BF16)
    zg_ref[:, 384:512] = jnp.where(lo, 0.0, qb).astype(BF16)
    k = proj(o + gw, o + gw + 128)
    kms = jnp.dot(k * k, seg[0:128, 0:128], precision=HIGHEST, preferred_element_type=F32)
    k = k * lax.rsqrt(kms + NORM_EPS) * kg_ref[...]
    gk_ref[...] = _rope(k, c64[:, 0:128], sa64[:, 0:128], sb64[:, 0:128], 32).astype(BF16)
    v = proj(o + gw + 128, o + gw + 256)
    gv_ref[0] = jnp.where(lo, v, 1.0).astype(BF16)
    gv_ref[1] = jnp.where(lo, 1.0, v).astype(BF16)

    o = 6 * gw
    zh_ref[...] = proj(o, o + 5 * gw)

    o = 11 * gw
    zd_ref[...] = (_rope(proj(o, o + gw), c32, sa32, sb32, 16) * (DIFF_QK_DIM ** -0.5 * LOG2E)).astype(BF16)
    kd_ref[...] = _rope(proj(o + gw, o + 2 * gw), c32, sa32, sb32, 16).astype(BF16)
    v = proj(o + 2 * gw, o + 3 * gw)
    va, vb = v[:, 0:128], v[:, 128:256]
    vd_ref[0] = jnp.where(lo, va, 1.0).astype(BF16)
    vd_ref[1] = jnp.where(lo, pltpu.roll(va, 64, 1), 1.0).astype(BF16)
    vd_ref[2] = jnp.where(lo, vb, 1.0).astype(BF16)
    vd_ref[3] = jnp.where(lo, pltpu.roll(vb, 64, 1), 1.0).astype(BF16)

    vt_ref[...] = lax.dot_general(wvt_ref[...], hb, NT_DIMS, preferred_element_type=F32).astype(BF16)


def _in_proj_call(x_all, mod_all, g1, w_bf, wvt_bf, tabs, qg, kg, seg, n_lat_rows, t_len):
    r, d = x_all.shape
    tm = ROW_TILE
    gw = GROUP_WIDTH
    n_lat_tiles = n_lat_rows // tm
    tiles_per_seq = t_len // tm
    n_batch = n_lat_rows // t_len

    def mod_idx(i):
        return (jnp.where(i < n_lat_tiles, i // tiles_per_seq, n_batch), 0, 0)

    def tab_idx(i):
        return (jnp.where(i < n_lat_tiles, i % tiles_per_seq, tiles_per_seq), 0)

    ctx_tiles = (r - n_lat_rows) // n_batch // tm
    seq_tiles = tiles_per_seq + ctx_tiles

    def kv_idx(i):
        ci = i - n_lat_tiles
        lat = (i // tiles_per_seq) * seq_tiles + ctx_tiles + i % tiles_per_seq
        ctx = (ci // ctx_tiles) * seq_tiles + ci % ctx_tiles
        return (jnp.where(i < n_lat_tiles, lat, ctx), 0)

    row = lambda i: (i, 0)
    const = lambda i: (0, 0)
    tab_spec = pl.BlockSpec((tm, gw), tab_idx)
    return pl.pallas_call(
        _in_proj_kernel,
        grid=(r // tm,),
        in_specs=[pl.BlockSpec((tm, d), row),
                  pl.BlockSpec((1, 1, mod_all.shape[-1]), mod_idx),
                  pl.BlockSpec((1, d), const),
                  pl.BlockSpec(w_bf.shape, const),
                  pl.BlockSpec(wvt_bf.shape, const),
                  tab_spec, tab_spec, tab_spec, tab_spec, tab_spec, tab_spec,
                  pl.BlockSpec((1, gw), const),
                  pl.BlockSpec((1, 128), const),
                  pl.BlockSpec((gw, gw), const)],
        out_specs=[pl.BlockSpec((tm, 4 * gw), row),
                   pl.BlockSpec((tm, 2 * gw), row),
                   pl.BlockSpec((tm, 5 * gw), row),
                   pl.BlockSpec((tm, gw), row),
                   pl.BlockSpec((2 * gw, tm), lambda i: (0, i)),
                   pl.BlockSpec((tm, 128), kv_idx),
                   pl.BlockSpec((GQA_KV_HEADS, tm, 128), lambda i: (0,) + kv_idx(i)),
                   pl.BlockSpec((tm, gw), kv_idx),
                   pl.BlockSpec((GROUP_HEADS, tm, 128), lambda i: (0,) + kv_idx(i))],
        out_shape=[jax.ShapeDtypeStruct((r, 4 * gw), F32),
                   jax.ShapeDtypeStruct((r, 2 * gw), BF16),
                   jax.ShapeDtypeStruct((r, 5 * gw), F32),
                   jax.ShapeDtypeStruct((r, gw), BF16),
                   jax.ShapeDtypeStruct((2 * gw, r), BF16),
                   jax.ShapeDtypeStruct((r, 128), BF16),
                   jax.ShapeDtypeStruct((GQA_KV_HEADS, r, 128), BF16),
                   jax.ShapeDtypeStruct((r, gw), BF16),
                   jax.ShapeDtypeStruct((GROUP_HEADS, r, 128), BF16)],
        compiler_params=_cparams("arbitrary"),
        name="in_proj",
    )(x_all, mod_all, g1, w_bf, wvt_bf, *tabs, qg, kg, seg)


def _rec_direction(q_ref, k_ref, v_ref, vt_ref, par, bd_ref, o_ref, s_ref,
                   lc_ref, qt_ref, kh_ref, kk_ref, tot_ref, a_ref, *, reverse):
    c, gw = q_ref.shape
    sub = REC_SUB
    n_sub = c // sub
    sig = _sigmoid(k_ref[...])
    logf = jnp.log(par + (1.0 - par) * sig)
    kk = (1.0 - par) * (1.0 - sig)
    r_i = lax.broadcasted_iota(jnp.int32, (c, c), 0)
    c_i = lax.broadcasted_iota(jnp.int32, (c, c), 1)
    shift = sub.bit_length() - 1
    same = jnp.right_shift(r_i, shift) == jnp.right_shift(c_i, shift)
    if reverse:
        incl = same & (c_i >= r_i)
        excl = same & (c_i < r_i)
    else:
        incl = same & (c_i <= r_i)
        excl = same & (c_i > r_i)
    lc = jnp.dot(jnp.where(incl, 1.0, 0.0), logf, precision=HIGHEST, preferred_element_type=F32)
    rr = jnp.dot(jnp.where(excl, 1.0, 0.0), logf, precision=HIGHEST, preferred_element_type=F32)
    lc_ref[...] = lc * LOG2E
    qt_ref[...] = (q_ref[...] * jnp.exp(lc)).astype(BF16)
    kh_ref[...] = kk * jnp.exp(rr)
    kk_ref[...] = kk
    tot_ref[...] = lc + rr
    bd = bd_ref[...]
    t_loc = lax.broadcasted_iota(jnp.int32, (sub, gw), 0)
    row_id = lax.broadcasted_iota(jnp.int32, (c, gw), 0)
    half = gw // 2
    head_shift = HEAD_DIM.bit_length() - 1
    same_head = (jnp.right_shift(lax.broadcasted_iota(jnp.int32, (half, half), 0), head_shift)
                 == jnp.right_shift(lax.broadcasted_iota(jnp.int32, (half, half), 1), head_shift))
    quads = (slice(0, half), slice(half, gw))

    def step(j):
        a = (n_sub - 1 - j) if reverse else j
        base = a * sub
        rows = pl.ds(base, sub)
        s_q = [s_ref[qd, qd] for qd in quads]
        qt_a = qt_ref[rows, :]
        o_inter = jnp.concatenate(
            [lax.dot_general(qt_a[:, qd], s.astype(BF16), NT_DIMS, preferred_element_type=F32)
             for qd, s in zip(quads, s_q)], axis=1)
        lc_a = lc_ref[rows, :]
        q_a = q_ref[rows, :]
        k_a = kk_ref[rows, :]
        v_a = v_ref[rows, :]
        for s in range(sub):
            valid = (t_loc <= s) if reverse else (t_loc >= s)
            e = jnp.where(valid, jnp.exp2(lc_a - lc_a[s:s + 1, :]), 0.0)
            a_ref[a, s * sub:(s + 1) * sub, :] = (q_a * e * k_a[s:s + 1, :]).astype(BF16)
        b = jnp.dot(a_ref[a], bd, preferred_element_type=F32)
        o_intra = jnp.zeros((sub, gw), F32)
        for s in range(sub):
            o_intra = o_intra + b[s * sub:(s + 1) * sub, :] * v_a[s:s + 1, :]
        o_ref[rows, :] = o_inter + o_intra
        in_sub = (row_id >= base) & (row_id < base + sub)
        kh_m = jnp.where(in_sub, kh_ref[...], 0.0).astype(BF16)
        decay = jnp.exp(tot_ref[pl.ds(base, 1), :])
        for qd, s in zip(quads, s_q):
            u = jnp.dot(vt_ref[qd, :], kh_m[:, qd], preferred_element_type=F32)
            s_ref[qd, qd] = s * decay[:, qd] + jnp.where(same_head, u, 0.0)

    return step, n_sub


def _rec_kernel(qf_ref, kf_ref, vf_ref, vtf_ref, qb_ref, kb_ref, vb_ref, vtb_ref, par_ref, bd_ref,
                of_ref, ob_ref, sf_ref, sb_ref, *tmp):
    @pl.when(pl.program_id(1) == 0)
    def _():
        sf_ref[...] = jnp.zeros_like(sf_ref)
        sb_ref[...] = jnp.zeros_like(sb_ref)

    n_tmp = len(tmp) // 2
    step_f, n_sub = _rec_direction(qf_ref, kf_ref, vf_ref, vtf_ref, par_ref[0:1, :], bd_ref, of_ref,
                                   sf_ref, *tmp[:n_tmp], reverse=False)
    step_b, _ = _rec_direction(qb_ref, kb_ref, vb_ref, vtb_ref, par_ref[1:2, :], bd_ref, ob_ref,
                               sb_ref, *tmp[n_tmp:], reverse=True)
    for j in range(n_sub):
        step_f(j)
        step_b(j)


def _ret_direction(q_ref, k_ref, v_ref, vt_ref, logg, o_ref, s_ref, *, reverse):
    c, gw = q_ref.shape
    q = q_ref[...]
    k = k_ref[...]
    t_row = lax.broadcasted_iota(jnp.int32, (c, gw), 0).astype(F32)
    if reverse:
        n_q, n_k = c - t_row, t_row
    else:
        n_q, n_k = t_row + 1.0, (c - 1.0) - t_row
    s_t = s_ref[...]
    qt = (q * jnp.exp(n_q * logg)).astype(BF16)
    o = lax.dot_general(qt, s_t.astype(BF16), NT_DIMS, preferred_element_type=F32)
    r_i = lax.broadcasted_iota(jnp.int32, (c, c), 0)
    c_i = lax.broadcasted_iota(jnp.int32, (c, c), 1)
    dist = (c_i - r_i) if reverse else (r_i - c_i)
    allowed = dist >= 0
    dist_f = dist.astype(F32)
    lane = lax.broadcasted_iota(jnp.int32, (c, gw), 1)
    kb = k.astype(BF16)
    vb = v_ref[...].astype(BF16)
    for h in range(GROUP_HEADS):
        in_h = (lane >= h * HEAD_DIM) & (lane < (h + 1) * HEAD_DIM)
        mask = jnp.where(allowed, jnp.exp(dist_f * logg[:, h * HEAD_DIM:h * HEAD_DIM + 1]), 0.0)
        qh = jnp.where(in_h, q, 0.0).astype(BF16)
        sc = lax.dot_general(qh, kb, NT_DIMS, preferred_element_type=F32) * mask
        oh = jnp.dot(sc.astype(BF16), vb, preferred_element_type=F32)
        o = o + jnp.where(in_h, oh, 0.0)
    o_ref[...] = o
    kh = (k * jnp.exp(n_k * logg)).astype(BF16)
    u = jnp.dot(vt_ref[...], kh, preferred_element_type=F32)
    head_shift = HEAD_DIM.bit_length() - 1
    same_head = (jnp.right_shift(lax.broadcasted_iota(jnp.int32, (gw, gw), 0), head_shift)
                 == jnp.right_shift(lax.broadcasted_iota(jnp.int32, (gw, gw), 1), head_shift))
    s_ref[...] = s_t * jnp.exp(float(c) * logg) + jnp.where(same_head, u, 0.0)


def _ret_kernel(qf_ref, kf_ref, vf_ref, vtf_ref, qb_ref, kb_ref, vb_ref, vtb_ref, par_ref, bd_ref,
                of_ref, ob_ref, sf_ref, sb_ref):
    @pl.when(pl.program_id(1) == 0)
    def _():
        sf_ref[...] = jnp.zeros_like(sf_ref)
        sb_ref[...] = jnp.zeros_like(sb_ref)

    _ret_direction(qf_ref, kf_ref, vf_ref, vtf_ref, par_ref[0:1, :], of_ref, sf_ref, reverse=False)
    _ret_direction(qb_ref, kb_ref, vb_ref, vtb_ref, par_ref[1:2, :], ob_ref, sb_ref, reverse=True)


def _rec_call(z, vt_all, par, bd, *, hgrn, n_batch, t_len, ctx_len, cols, vt_row):
    r = z.shape[0]
    c = REC_CHUNK
    gw = GROUP_WIDTH
    n_ctx = ctx_len // c
    n_lat = t_len // c
    lat_blocks = n_batch * n_lat

    def fwd_blk(b, i):
        return jnp.where(i < n_ctx, lat_blocks + b * n_ctx + i, b * n_lat + i - n_ctx)

    def bwd_blk(b, i):
        return jnp.where(i < n_ctx, lat_blocks + b * n_ctx + (n_ctx - 1 - i),
                         b * n_lat + (n_lat - 1 - (i - n_ctx)))

    def zspec(blk, col):
        return pl.BlockSpec((c, gw), lambda b, i: (blk(b, i), col))

    def vtspec(blk):
        return pl.BlockSpec((gw, c), lambda b, i: (vt_row, blk(b, i)))

    const = lambda b, i: (0, 0)
    cq, ckf, ckb, cv = cols
    state_scratch = [pltpu.VMEM((gw, gw), F32), pltpu.VMEM((gw, gw), F32)]
    if hgrn:
        body = _rec_kernel
        scratch = state_scratch + 2 * [
            pltpu.VMEM((c, gw), F32), pltpu.VMEM((c, gw), BF16), pltpu.VMEM((c, gw), F32),
            pltpu.VMEM((c, gw), F32), pltpu.VMEM((c, gw), F32),
            pltpu.VMEM((c // REC_SUB, REC_SUB * REC_SUB, gw), BF16)]
    else:
        body = _ret_kernel
        scratch = state_scratch
    out_f, out_b = pl.pallas_call(
        body,
        grid=(n_batch, n_ctx + n_lat),
        in_specs=[zspec(fwd_blk, cq), zspec(fwd_blk, ckf), zspec(fwd_blk, cv), vtspec(fwd_blk),
                  zspec(bwd_blk, cq), zspec(bwd_blk, ckb), zspec(bwd_blk, cv), vtspec(bwd_blk),
                  pl.BlockSpec((2, gw), const),
                  pl.BlockSpec((gw, gw), const)],
        out_specs=[pl.BlockSpec((c, gw), lambda b, i: (fwd_blk(b, i), 0)),
                   pl.BlockSpec((c, gw), lambda b, i: (bwd_blk(b, i), 0))],
        out_shape=[jax.ShapeDtypeStruct((r, gw), F32), jax.ShapeDtypeStruct((r, gw), F32)],
        scratch_shapes=scratch,
        compiler_params=_cparams("arbitrary", "arbitrary"),
        name="hgrn_rec" if hgrn else "ret_rec",
    )(z, z, z, vt_all, z, z, z, vt_all, par, bd)
    return out_f, out_b


def _online_softmax(lhs_ref, k_ref, v_ref, kv_chunk, scr):
    m_ref, acc_ref = scr[0:2]
    s_refs, p_refs, a_refs, x_refs = scr[2:4], scr[4:6], scr[6:8], scr[8:10]
    n_heads = lhs_ref.shape[0]
    n = k_ref.shape[0] // kv_chunk
    steps = n_heads * n
    assert steps % 2 == 0 and steps >= 4
    m_ref[...] = jnp.full(m_ref.shape, -jnp.inf, F32)
    acc_ref[...] = jnp.zeros(acc_ref.shape, F32)

    def split(t):
        if isinstance(t, int):
            return t // n, pl.ds((t % n) * kv_chunk, kv_chunk)
        h = t // n
        return h, pl.ds(pl.multiple_of((t - h * n) * kv_chunk, kv_chunk), kv_chunk)

    def qk(t, slot):
        h, rows = split(t)
        s = lax.dot_general(lhs_ref[h], k_ref[rows, :], NT_DIMS, preferred_element_type=F32)
        s_refs[slot][...] = s
        x_refs[slot][...] = jnp.max(s, axis=-1, keepdims=True)

    def sm(t, slot):
        h, _ = split(t)
        s = s_refs[slot][...]
        m = m_ref[h]
        m_new = jnp.maximum(m, x_refs[slot][...])
        m_ref[h] = m_new
        a_refs[slot][...] = jnp.exp2(m - m_new)
        p_refs[slot][...] = jnp.exp2((s - m_new).astype(BF16))

    def pv(t, slot):
        h, rows = split(t)
        v_blk = v_ref[rows, :] if len(v_ref.shape) == 2 else v_ref[h, rows, :]
        acc_ref[h] = a_refs[slot][...] * acc_ref[h] + jnp.dot(
            p_refs[slot][...], v_blk, preferred_element_type=F32)

    qk(0, 0)
    qk(1, 1)
    sm(0, 0)

    def body(i, carry):
        t = 2 * i + 1
        qk(t + 1, 0)
        sm(t, 1)
        pv(t - 1, 0)
        qk(t + 2, 1)
        sm(t + 1, 0)
        pv(t, 1)
        return carry

    lax.fori_loop(0, (steps - 2) // 2, body, 0)
    sm(steps - 1, 1)
    pv(steps - 2, 0)
    pv(steps - 1, 1)


def _gqa_kernel(q_ref, k_ref, v_ref, o_ref, lhs_ref, *scr, kv_chunk):
    tq = q_ref.shape[0]
    lane = lax.broadcasted_iota(jnp.int32, (tq, 128), 1)
    lo = lane < HEAD_DIM
    for j in range(GQA_KV_HEADS):
        lhs_ref[j, 0:tq, :] = q_ref[:, (2 * j) * 128:(2 * j + 1) * 128]
        lhs_ref[j, tq:2 * tq, :] = q_ref[:, (2 * j + 1) * 128:(2 * j + 2) * 128]
    _online_softmax(lhs_ref, k_ref, v_ref, kv_chunk, scr)
    acc_ref = scr[1]
    lo2 = lax.broadcasted_iota(jnp.int32, (2 * tq, 128), 1) < HEAD_DIM
    acc0, acc1 = acc_ref[0], acc_ref[1]
    o0 = acc0 / jnp.where(lo2, pltpu.roll(acc0, 64, 1), 1.0)
    o1 = acc1 / jnp.where(lo2, 1.0, pltpu.roll(acc1, 64, 1))
    o_ref[:, 0:128] = jnp.where(lo, o0[0:tq], pltpu.roll(o0[tq:2 * tq], 64, 1))
    o_ref[:, 128:256] = jnp.where(lo, pltpu.roll(o1[0:tq], 64, 1), o1[tq:2 * tq])


def _diff_kernel(lam_ref, q_ref, k_ref, v_ref, ng_ref, seg_ref, o_ref, lhs_ref, *scr, kv_chunk,
                 out_scale):
    tq, gw = q_ref.shape
    lane = lax.broadcasted_iota(jnp.int32, (tq, gw), 1)
    q = q_ref[...].astype(F32)
    lam = lam_ref[0, 0]
    for h in range(GROUP_HEADS):
        base = h * HEAD_DIM
        q1 = jnp.where((lane >= base) & (lane < base + DIFF_QK_DIM), q, 0.0)
        q2 = jnp.where((lane >= base + DIFF_QK_DIM) & (lane < base + HEAD_DIM), q, 0.0)
        lhs_ref[h, 0:tq, :] = q1.astype(BF16)
        lhs_ref[h, tq:2 * tq, :] = q2.astype(BF16)
    _online_softmax(lhs_ref, k_ref, v_ref, kv_chunk, scr)
    acc_ref = scr[1]
    lo = lax.broadcasted_iota(jnp.int32, (tq, 128), 1) < HEAD_DIM
    lo2 = lax.broadcasted_iota(jnp.int32, (2 * tq, 128), 1) < HEAD_DIM
    heads = []
    for h in range(GROUP_HEADS):
        acc = acc_ref[h]
        o = acc / jnp.where(lo2, pltpu.roll(acc, 64, 1), 1.0)
        heads.append(o[0:tq] - lam * o[tq:2 * tq])
    out = jnp.concatenate([jnp.where(lo, heads[0], pltpu.roll(heads[1], 64, 1)),
                           jnp.where(lo, heads[2], pltpu.roll(heads[3], 64, 1))], axis=1)
    ms = jnp.dot(out * out, seg_ref[...], precision=HIGHEST, preferred_element_type=F32)
    o_ref[...] = out * lax.rsqrt(ms + NORM_EPS) * ng_ref[...] * out_scale


def _kv_chunk(s_len, n_heads):
    best = None
    for kc in range(128, min(s_len, ATT_KV_CHUNK_MAX) + 1, 128):
        if s_len % kc == 0 and n_heads * (s_len // kc) >= 4:
            best = kc
    assert best is not None, s_len
    return best


def _attn_call(kind, zq, kv, *, n_batch, t_len, ctx_len, ctx_queries, lam=None, norm_g=None,
               seg=None, out_scale=None):
    gw = GROUP_WIDTH
    lat_rows = n_batch * t_len
    s_all = ctx_len + t_len
    if kind == "gqa":
        qw, kw, n_heads = 512, 128, GQA_KV_HEADS
    else:
        qw, kw, n_heads = 256, 256, GROUP_HEADS
    if ctx_queries:
        tq, s_len = ctx_len, ctx_len
        grid = (n_batch, 1)
        q_map = lambda b, i: (lat_rows // ctx_len + b, 0)
        out_map = lambda b, i: (b, 0)
        kv_blk = lambda b: b * (s_all // ctx_len)
    else:
        tq, s_len = ATT_Q_TILE, s_all
        grid = (n_batch, t_len // tq)
        q_map = lambda b, i: (b * (t_len // tq) + i, 0)
        out_map = q_map
        kv_blk = lambda b: b
    kc = _kv_chunk(s_len, n_heads)
    nv = 128
    kv_specs = [pl.BlockSpec((s_len, kw), lambda b, i: (kv_blk(b), 0)),
                pl.BlockSpec((n_heads, s_len, nv), lambda b, i: (0, kv_blk(b), 0))]
    kv_args = kv
    out_spec = pl.BlockSpec((tq, gw), out_map)
    out_shape = jax.ShapeDtypeStruct((n_batch * tq * grid[1], gw), F32)
    q_spec = pl.BlockSpec((tq, qw), q_map)
    const = lambda b, i: (0, 0)
    m_rows = 2 * tq
    scratch = [pltpu.VMEM((n_heads, m_rows, kw), BF16),
               pltpu.VMEM((n_heads, m_rows, 1), F32),
               pltpu.VMEM((n_heads, m_rows, nv), F32),
               pltpu.VMEM((m_rows, kc), F32), pltpu.VMEM((m_rows, kc), F32),
               pltpu.VMEM((m_rows, kc), BF16), pltpu.VMEM((m_rows, kc), BF16),
               pltpu.VMEM((m_rows, 1), F32), pltpu.VMEM((m_rows, 1), F32),
               pltpu.VMEM((m_rows, 1), F32), pltpu.VMEM((m_rows, 1), F32)]
    if kind == "gqa":
        return pl.pallas_call(
            functools.partial(_gqa_kernel, kv_chunk=kc),
            grid=grid, in_specs=[q_spec] + kv_specs, out_specs=out_spec, out_shape=out_shape,
            scratch_shapes=scratch,
            compiler_params=_cparams("arbitrary", "arbitrary"),
            name="gqa_ctx" if ctx_queries else "gqa_lat",
        )(zq, *kv_args)
    return pl.pallas_call(
        functools.partial(_diff_kernel, kv_chunk=kc, out_scale=out_scale),
        grid=grid,
        in_specs=[pl.BlockSpec(memory_space=pltpu.SMEM), q_spec] + kv_specs
        + [pl.BlockSpec((1, gw), const), pl.BlockSpec((gw, gw), const)],
        out_specs=out_spec, out_shape=out_shape,
        scratch_shapes=scratch,
        compiler_params=_cparams("arbitrary", "arbitrary"),
        name="diff_ctx" if ctx_queries else "diff_lat",
    )(lam, zq, *kv_args, norm_g, seg)


def _out_proj_kernel(rf_ref, rb_ref, rg_ref, ga_ref, hf_ref, hb_ref, hg_ref, df_ref, x_ref, mod_ref,
                     w_ref, rn_ref, hn_ref, seg_ref, n2_ref, rw_ref, rbias_ref,
                     xo_ref, h2_ref, e_ref, gt_ref):
    sub = OUT_SUB_TILE
    for i0 in range(0, x_ref.shape[0], sub):
        _out_proj_rows(slice(i0, i0 + sub), rf_ref, rb_ref, rg_ref, ga_ref, hf_ref, hb_ref, hg_ref,
                       df_ref, x_ref, mod_ref, w_ref, rn_ref, hn_ref, seg_ref, n2_ref, rw_ref,
                       rbias_ref, xo_ref, h2_ref, e_ref, gt_ref)


def _out_proj_rows(rs, rf_ref, rb_ref, rg_ref, ga_ref, hf_ref, hb_ref, hg_ref, df_ref, x_ref, mod_ref,
                   w_ref, rn_ref, hn_ref, seg_ref, n2_ref, rw_ref, rbias_ref,
                   xo_ref, h2_ref, e_ref, gt_ref):
    d = x_ref.shape[-1]
    gw = GROUP_WIDTH
    seg = seg_ref[...]
    mod = mod_ref[0]

    def readout(o, norm_g, gate):
        ms = jnp.dot(o * o, seg, precision=HIGHEST, preferred_element_type=F32)
        return o * lax.rsqrt(ms + NORM_EPS) * norm_g * (gate * _sigmoid(gate))

    parts = (readout(rf_ref[rs, :] + rb_ref[rs, :], rn_ref[...], rg_ref[rs, :]),
             ga_ref[rs, :],
             readout(hf_ref[rs, :] + hb_ref[rs, :], hn_ref[...], hg_ref[rs, :]),
             df_ref[rs, :])
    acc = jnp.zeros((rs.stop - rs.start, d), F32)
    for n, part in enumerate(parts):
        acc = acc + jnp.dot(part.astype(BF16), w_ref[n * gw:(n + 1) * gw, :],
                            preferred_element_type=F32)
    x = x_ref[rs, :] + mod[:, 2 * d:3 * d] * acc
    xo_ref[rs, :] = x
    ms = jnp.mean(x * x, axis=-1, keepdims=True)
    h2 = x * lax.rsqrt(ms + NORM_EPS) * n2_ref[...]
    h2 = h2 * (1.0 + mod[:, 4 * d:5 * d]) + mod[:, 3 * d:4 * d]
    h2_ref[rs, :] = h2

    logits = lax.dot_general(rw_ref[...], h2, NT_DIMS, precision=HIGHEST, preferred_element_type=F32)
    score = _sigmoid(logits)
    sel = score + rbias_ref[:, 0:1]
    ng = N_EXPERT_GROUPS
    sc = [score[m * ng:(m + 1) * ng, :] for m in range(EXPERTS_PER_GROUP)]
    sl = [sel[m * ng:(m + 1) * ng, :] for m in range(EXPERTS_PER_GROUP)]
    hi1, lo1 = jnp.maximum(sl[0], sl[1]), jnp.minimum(sl[0], sl[1])
    hi2, lo2 = jnp.maximum(sl[2], sl[3]), jnp.minimum(sl[2], sl[3])
    group_score = jnp.maximum(hi1, hi2) + jnp.maximum(jnp.minimum(hi1, hi2), jnp.maximum(lo1, lo2))
    g_iota = lax.broadcasted_iota(jnp.int32, group_score.shape, 0).astype(F32)
    g_max = jnp.max(group_score, axis=0, keepdims=True)
    best = jnp.min(jnp.where(group_score == g_max, g_iota, float(ng)), axis=0, keepdims=True)
    pick = g_iota == best
    v = [jnp.sum(jnp.where(pick, a, 0.0), axis=0, keepdims=True) for a in sl]
    u = [jnp.sum(jnp.where(pick, a, 0.0), axis=0, keepdims=True) for a in sc]
    rank = []
    for i in range(EXPERTS_PER_GROUP):
        r_i = jnp.zeros(best.shape, F32)
        for j in range(EXPERTS_PER_GROUP):
            if j < i:
                r_i = r_i + jnp.where(v[j] >= v[i], 1.0, 0.0)
            elif j > i:
                r_i = r_i + jnp.where(v[j] > v[i], 1.0, 0.0)
        rank.append(r_i)
    idx, gate = [], []
    for kth in range(2):
        idx.append(sum(jnp.where(rank[i] == kth, float(i), 0.0) for i in range(EXPERTS_PER_GROUP)))
        gate.append(sum(jnp.where(rank[i] == kth, u[i], 0.0) for i in range(EXPERTS_PER_GROUP)))
    tot = gate[0] + gate[1]
    e_ref[0:1, rs] = (best * EXPERTS_PER_GROUP + idx[0]).astype(jnp.int32)
    e_ref[1:2, rs] = (best * EXPERTS_PER_GROUP + idx[1]).astype(jnp.int32)
    gt_ref[0:1, rs] = gate[0] / tot
    gt_ref[1:2, rs] = gate[1] / tot


def _out_proj_call(rf, rb, zr, o_gqa, hf, hb, zh, o_df, x_all, mod_all, w_out_bf, rn, hn, seg, n2,
                   rw_t, rbias, n_rows, n_lat_rows, t_len):
    d = x_all.shape[-1]
    tm = OUT_TILE
    gw = GROUP_WIDTH
    n_lat_tiles = n_lat_rows // tm
    tiles_per_seq = t_len // tm
    n_batch = n_lat_rows // t_len

    def mod_idx(i):
        return (jnp.where(i < n_lat_tiles, i // tiles_per_seq, n_batch), 0, 0)

    row = lambda i: (i, 0)
    const = lambda i: (0, 0)
    blk = lambda col: pl.BlockSpec((tm, gw), lambda i: (i, col))
    return pl.pallas_call(
        _out_proj_kernel,
        grid=(n_rows // tm,),
        in_specs=[blk(0), blk(0), blk(3), blk(0), blk(0), blk(0), blk(4), blk(0),
                  pl.BlockSpec((tm, d), row),
                  pl.BlockSpec((1, 1, mod_all.shape[-1]), mod_idx),
                  pl.BlockSpec((4 * gw, d), const),
                  pl.BlockSpec((1, gw), const), pl.BlockSpec((1, gw), const),
                  pl.BlockSpec((gw, gw), const),
                  pl.BlockSpec((1, d), const),
                  pl.BlockSpec((N_EXPERTS, d), const),
                  pl.BlockSpec((N_EXPERTS, 128), const)],
        out_specs=[pl.BlockSpec((tm, d), row), pl.BlockSpec((tm, d), row),
                   pl.BlockSpec((2, tm), lambda i: (0, i)), pl.BlockSpec((2, tm), lambda i: (0, i))],
        out_shape=[jax.ShapeDtypeStruct((n_rows, d), F32), jax.ShapeDtypeStruct((n_rows, d), F32),
                   jax.ShapeDtypeStruct((2, n_rows), jnp.int32),
                   jax.ShapeDtypeStruct((2, n_rows), F32)],
        compiler_params=_cparams("arbitrary"),
        name="out_proj",
    )(rf, rb, zr, o_gqa, hf, hb, zh, o_df, x_all, mod_all, w_out_bf, rn, hn, seg, n2, rw_t, rbias)


def _expert_kernel(be_ref, nb_ref, idx_hbm, gate_ref, h2_hbm, wg_ref, wu_ref, wd_ref, ys_hbm,
                   idx_ref, x_ref, y_ref, wgb_ref, wub_ref, wdb_ref, sem_i, sem_g, sem_s):
    i = pl.program_id(0)
    nb = nb_ref[0]
    rows = MOE_BLOCK

    def idx_copy(blk):
        return pltpu.make_async_copy(idx_hbm.at[blk], idx_ref.at[blk % 3], sem_i.at[blk % 3])

    def issue_gather(blk):
        islot, xslot = blk % 3, blk % 2
        for r in range(rows):
            pltpu.make_async_copy(h2_hbm.at[pl.ds(idx_ref[islot, r], 1)],
                                  x_ref.at[xslot, pl.ds(r, 1)], sem_g.at[xslot]).start()

    def wait_slot(buf_ref, sem, slot):
        pltpu.make_async_copy(buf_ref.at[slot], buf_ref.at[slot], sem.at[slot]).wait()

    @pl.when(i < nb)
    def _():
        slot = i % 2

        @pl.when(i == 0)
        def _():
            y_ref[...] = jnp.zeros(y_ref.shape, F32)
            n_real = ys_hbm.shape[0] - 2 * rows
            for s in range(2):
                fill = pltpu.make_async_copy(y_ref.at[s], ys_hbm.at[pl.ds(n_real + s * rows, rows)],
                                             sem_s.at[s])
                fill.start()
                fill.wait()
            idx_copy(0).start()
            idx_copy(0).wait()
            issue_gather(0)

            @pl.when(nb > 1)
            def _():
                idx_copy(1).start()

        @pl.when(i + 1 < nb)
        def _():
            idx_copy(i + 1).wait()
            issue_gather(i + 1)

            @pl.when(i + 2 < nb)
            def _():
                idx_copy(i + 2).start()

        wait_slot(x_ref, sem_g, slot)

        @pl.when(i >= 2)
        def _():
            wait_slot(y_ref, sem_s, slot)

        @pl.when((i == 0) | (be_ref[i] != be_ref[jnp.maximum(i - 1, 0)]))
        def _():
            wgb_ref[...] = wg_ref[0].astype(BF16)
            wub_ref[...] = wu_ref[0].astype(BF16)
            wdb_ref[...] = wd_ref[0].astype(BF16)

        xb = x_ref[slot].astype(BF16)
        a = jnp.dot(xb, wgb_ref[...], preferred_element_type=F32)
        u = jnp.dot(xb, wub_ref[...], preferred_element_type=F32)
        hmid = (a * _sigmoid(a) * u).astype(BF16)
        y_ref[slot] = jnp.dot(hmid, wdb_ref[...], preferred_element_type=F32) * gate_ref[:, 0:1]

        islot = i % 3
        for r in range(rows):
            pltpu.make_async_copy(y_ref.at[slot, pl.ds(r, 1)],
                                  ys_hbm.at[pl.ds(idx_ref[islot, rows + r], 1)], sem_s.at[slot]).start()

        @pl.when(i == nb - 1)
        def _():
            @pl.when(i >= 1)
            def _():
                wait_slot(y_ref, sem_s, 1 - slot)
            wait_slot(y_ref, sem_s, slot)


def _expert_call(block_expert, n_used, row_idx, row_gate, h2, wg, wu, wd, n_tok, layer):
    n_blocks = block_expert.shape[0]
    d = h2.shape[-1]
    ff = wg.shape[-1]
    rows = MOE_BLOCK
    grid_spec = pltpu.PrefetchScalarGridSpec(
        num_scalar_prefetch=2,
        grid=(n_blocks,),
        in_specs=[pl.BlockSpec(memory_space=pl.ANY),
                  pl.BlockSpec((rows, 1), lambda i, be, nb: (i, 0)),
                  pl.BlockSpec(memory_space=pl.ANY),
                  pl.BlockSpec((None, 1, d, ff), lambda i, be, nb: (layer, be[i], 0, 0)),
                  pl.BlockSpec((None, 1, d, ff), lambda i, be, nb: (layer, be[i], 0, 0)),
                  pl.BlockSpec((None, 1, ff, d), lambda i, be, nb: (layer, be[i], 0, 0))],
        out_specs=pl.BlockSpec(memory_space=pl.ANY),
        scratch_shapes=[pltpu.SMEM((3, 2 * rows), jnp.int32),
                        pltpu.VMEM((2, rows, d), F32),
                        pltpu.VMEM((2, rows, d), F32),
                        pltpu.VMEM((d, ff), BF16), pltpu.VMEM((d, ff), BF16), pltpu.VMEM((ff, d), BF16),
                        pltpu.SemaphoreType.DMA((3,)), pltpu.SemaphoreType.DMA((2,)),
                        pltpu.SemaphoreType.DMA((2,))],
    )
    return pl.pallas_call(
        _expert_kernel,
        grid_spec=grid_spec,
        out_shape=jax.ShapeDtypeStruct((2 * n_tok + 2 * rows, d), F32),
        compiler_params=_cparams("arbitrary"),
        name="experts",
    )(block_expert, n_used, row_idx, row_gate, h2, wg, wu, wd)


def _dispatch_plan(e_t, g_t, n_tok):
    m = 2 * n_tok
    e_flat = e_t.reshape(m)
    order = jnp.argsort(e_flat).astype(jnp.int32)
    experts = jnp.arange(N_EXPERTS, dtype=jnp.int32)
    counts = jnp.sum((e_flat[:, None] == experts[None, :]).astype(jnp.int32), axis=0)
    padded = (counts + MOE_BLOCK - 1) // MOE_BLOCK * MOE_BLOCK
    start = jnp.cumsum(counts) - counts
    pend = jnp.cumsum(padded)
    pstart = pend - padded
    n_blocks = -(-m // MOE_BLOCK) + N_EXPERTS
    blk_row0 = jnp.arange(n_blocks, dtype=jnp.int32) * MOE_BLOCK
    block_expert = jnp.minimum(
        jnp.sum((pend[None, :] <= blk_row0[:, None]).astype(jnp.int32), axis=1), N_EXPERTS - 1)
    blk = jnp.arange(n_blocks, dtype=jnp.int32)[:, None]
    r_in = jnp.arange(MOE_BLOCK, dtype=jnp.int32)[None, :]
    pos = blk_row0[:, None] + r_in - pstart[block_expert][:, None]
    valid = pos < counts[block_expert][:, None]
    sorted_pos = jnp.clip(start[block_expert][:, None] + pos, 0, m - 1)
    assign = order[sorted_pos]
    dump = m + (blk % 2) * MOE_BLOCK + r_in
    row_src = jnp.where(valid, assign % n_tok, 0)
    row_dst = jnp.where(valid, assign, dump)
    row_idx = jnp.concatenate([row_src, row_dst], axis=1).astype(jnp.int32)
    row_gate = jnp.where(valid, g_t.reshape(m)[assign], 0.0)
    n_used = (pend[-1] // MOE_BLOCK).astype(jnp.int32).reshape(1)
    return block_expert.astype(jnp.int32), n_used, row_idx, row_gate.reshape(n_blocks * MOE_BLOCK, 1)


def _combine_kernel(x_ref, y0_ref, y1_ref, mod_ref, fg_ref, o_ref, *, final_norm):
    d = x_ref.shape[-1]
    mod = mod_ref[0]
    x = x_ref[...] + mod[:, 5 * d:6 * d] * (y0_ref[...] + y1_ref[...])
    if final_norm:
        ms = jnp.mean(x * x, axis=-1, keepdims=True)
        x = x * lax.rsqrt(ms + NORM_EPS) * fg_ref[...]
    o_ref[...] = x


def _combine_call(x_new, ys, mod_all, final_g, n_rows, n_lat_rows, t_len, final_norm):
    d = x_new.shape[-1]
    tm = ROW_TILE
    n_lat_tiles = n_lat_rows // tm
    tiles_per_seq = t_len // tm
    n_batch = n_lat_rows // t_len
    n_tiles = n_rows // tm

    def mod_idx(i):
        return (jnp.where(i < n_lat_tiles, i // tiles_per_seq, n_batch), 0, 0)

    return pl.pallas_call(
        functools.partial(_combine_kernel, final_norm=final_norm),
        grid=(n_tiles,),
        in_specs=[pl.BlockSpec((tm, d), lambda i: (i, 0)),
                  pl.BlockSpec((tm, d), lambda i: (i, 0)),
                  pl.BlockSpec((tm, d), lambda i: (i + n_tiles, 0)),
                  pl.BlockSpec((1, 1, mod_all.shape[-1]), mod_idx),
                  pl.BlockSpec((1, d), lambda i: (0, 0))],
        out_specs=pl.BlockSpec((tm, d), lambda i: (i, 0)),
        out_shape=jax.ShapeDtypeStruct((n_rows, d), F32),
        compiler_params=_cparams("arbitrary"),
        name="combine",
    )(x_new, ys, ys, mod_all, final_g)


def _rope_tables(t_len, rot_dim, pad_rows):
    n_freq = rot_dim // 4
    half = rot_dim // 2
    inv_freq = ROPE_BASE ** (-jnp.arange(n_freq, dtype=F32) / n_freq)
    pos = jnp.arange(t_len)
    row = (pos // GRID_W).astype(F32)
    col = (pos % GRID_W).astype(F32)
    ang = jnp.concatenate([row[:, None] * inv_freq, col[:, None] * inv_freq], axis=-1)
    cos, sin = jnp.cos(ang), jnp.sin(ang)
    reps = GROUP_WIDTH // rot_dim
    zeros = jnp.zeros_like(sin)
    cos_f = jnp.tile(jnp.concatenate([cos, cos], -1), (1, reps))
    sin_a = jnp.tile(jnp.concatenate([-sin, zeros], -1), (1, reps))
    sin_b = jnp.tile(jnp.concatenate([zeros, sin], -1), (1, reps))
    pad1 = jnp.ones((pad_rows, GROUP_WIDTH), F32)
    pad0 = jnp.zeros((pad_rows, GROUP_WIDTH), F32)
    return (jnp.concatenate([cos_f, pad1], 0), jnp.concatenate([sin_a, pad0], 0),
            jnp.concatenate([sin_b, pad0], 0))


def kernel(x, c, ctx, c_ctx, w_mod, b_mod, norm1_g, norm2_g, w_in, ret_decay_logit, ret_norm_g,
           gqa_qnorm_g, gqa_knorm_g, hgrn_lb_logit, hgrn_norm_g, diff_lambda, diff_norm_g, w_out,
           router_w, router_bias, moe_w_gate, moe_w_up, moe_w_down, final_norm_g):
    n_batch, t_len, d = x.shape
    ctx_len = ctx.shape[1]
    depth = w_mod.shape[0]
    gw = GROUP_WIDTH
    lat_rows = n_batch * t_len
    all_rows = lat_rows + n_batch * ctx_len
    assert t_len % ROW_TILE == 0 and ctx_len % ROW_TILE == 0 and t_len % ATT_Q_TILE == 0
    assert t_len % OUT_TILE == 0 and (n_batch * ctx_len) % OUT_TILE == 0
    assert ctx_len % REC_CHUNK == 0 and (ctx_len + t_len) % ctx_len == 0 and n_batch + 1 <= 8

    x_all = jnp.concatenate([x.reshape(lat_rows, d), ctx.reshape(n_batch * ctx_len, d)], axis=0)
    cvec = jnp.zeros((8, d), F32).at[:n_batch].set(c).at[n_batch].set(c_ctx)

    tabs = _rope_tables(t_len, HEAD_DIM, ROW_TILE) + _rope_tables(t_len, DIFF_QK_DIM, ROW_TILE)
    head_of = jnp.arange(gw) // HEAD_DIM
    same_head = head_of[:, None] == head_of[None, :]
    seg = same_head.astype(F32) / HEAD_DIM
    bd = same_head.astype(BF16)

    sm = jax.nn.softmax(hgrn_lb_logit.astype(F32), axis=1)
    lower_bounds = jnp.cumsum(sm, axis=1) - sm[:, :1]
    log_gamma = jax.nn.log_sigmoid(ret_decay_logit.astype(F32))

    perm = (jnp.arange(N_EXPERTS) % N_EXPERT_GROUPS) * EXPERTS_PER_GROUP + jnp.arange(N_EXPERTS) // N_EXPERT_GROUPS
    rw_t = router_w.T[perm]
    rbias = jnp.broadcast_to(router_bias.astype(F32)[perm][:, None], (N_EXPERTS, 128))

    tile = lambda g: jnp.tile(g.astype(F32), GROUP_HEADS).reshape(1, gw)

    out = None
    for layer in range(depth):
        need_ctx = layer < depth - 1
        lambda_init = 0.8 - 0.6 * math.exp(-0.3 * layer)
        mod_all = _mod_call(cvec, w_mod[layer], b_mod[layer]).reshape(8, 1, 6 * d)

        w_l = w_in[layer]
        w_bf = w_l.astype(BF16)
        wvt_bf = jnp.concatenate([w_l[:, 2 * gw:3 * gw], w_l[:, 9 * gw:10 * gw]], axis=1).T.astype(BF16)
        zr, zg, zh, zd, vt_all, kg, vg, kd, vd = _in_proj_call(
            x_all, mod_all, norm1_g[layer].reshape(1, d), w_bf, wvt_bf, tabs,
            tile(gqa_qnorm_g[layer]), tile(gqa_knorm_g[layer])[:, :128], seg, lat_rows, t_len)

        rec_kw = dict(n_batch=n_batch, t_len=t_len, ctx_len=ctx_len)
        ret_par = jnp.repeat(log_gamma[layer], HEAD_DIM, axis=-1)
        rf, rb = _rec_call(zr, vt_all, ret_par, bd, hgrn=False, cols=(0, 1, 1, 2), vt_row=0, **rec_kw)
        hf, hb = _rec_call(zh, vt_all, lower_bounds[:, layer], bd, hgrn=True, cols=(0, 1, 2, 3),
                           vt_row=1, **rec_kw)

        lp = diff_lambda[layer].astype(F32)
        lam = (jnp.exp(jnp.sum(lp[0] * lp[1])) - jnp.exp(jnp.sum(lp[2] * lp[3])) + lambda_init)
        lam = lam.reshape(1, 1)
        att_kw = dict(n_batch=n_batch, t_len=t_len, ctx_len=ctx_len)
        diff_kw = dict(lam=lam, norm_g=tile(diff_norm_g[layer]), seg=seg, out_scale=1.0 - lambda_init)
        o_gqa = _attn_call("gqa", zg, (kg, vg), ctx_queries=False, **att_kw)
        o_df = _attn_call("diff", zd, (kd, vd), ctx_queries=False, **att_kw, **diff_kw)
        n_rows = lat_rows
        if need_ctx:
            o_gqa = jnp.concatenate(
                [o_gqa, _attn_call("gqa", zg, (kg, vg), ctx_queries=True, **att_kw)], 0)
            o_df = jnp.concatenate(
                [o_df, _attn_call("diff", zd, (kd, vd), ctx_queries=True, **att_kw, **diff_kw)], 0)
            n_rows = all_rows

        x_new, h2, e_t, g_t = _out_proj_call(
            rf, rb, zr, o_gqa, hf, hb, zh, o_df, x_all, mod_all, w_out[layer].astype(BF16),
            tile(ret_norm_g[layer]), tile(hgrn_norm_g[layer]), seg, norm2_g[layer].reshape(1, d),
            rw_t, rbias, n_rows, lat_rows, t_len)

        plan = _dispatch_plan(e_t, g_t, n_rows)
        ys = _expert_call(*plan, h2, moe_w_gate, moe_w_up, moe_w_down, n_rows, layer)
        last = layer == depth - 1
        x_next = _combine_call(x_new, ys, mod_all, final_norm_g.reshape(1, d), n_rows, lat_rows, t_len,
                               final_norm=last)
        if last:
            out = x_next[:lat_rows].reshape(n_batch, t_len, d)
        else:
            x_all = x_next
    return out
```

```python
import functools
import math

import jax
import jax.numpy as jnp
from jax import lax
from jax.experimental import pallas as pl
from jax.experimental.pallas import tpu as pltpu

F32 = jnp.float32
BF16 = jnp.bfloat16
HIGHEST = lax.Precision.HIGHEST

HEAD_DIM = 64
GROUP_HEADS = 4
GROUP_WIDTH = GROUP_HEADS * HEAD_DIM
GQA_KV_HEADS = 2
DIFF_QK_DIM = HEAD_DIM // 2
GRID_W = 64
ROPE_BASE = 10000.0
N_EXPERTS = 32
N_EXPERT_GROUPS = 8
EXPERTS_PER_GROUP = 4
MOE_BLOCK = 256
NORM_EPS = 1e-6

ROW_TILE = 256
OUT_TILE = 512
OUT_SUB_TILE = 128
REC_CHUNK = 128
REC_SUB = 16
ATT_Q_TILE = {"gqa": 512, "diff": 256}
ATT_KV_CHUNK_MAX = 1536
VMEM_LIMIT = 56 * 1024 * 1024

NT_DIMS = (((1,), (1,)), ((), ()))
LOG2E = math.log2(math.e)


def _sigmoid(v):
    return 1.0 / (1.0 + jnp.exp(-v))


def _cparams(*sem):
    return pltpu.CompilerParams(dimension_semantics=sem, vmem_limit_bytes=VMEM_LIMIT)


def _mod_kernel(c_ref, w_ref, b_ref, o_ref):
    cv = c_ref[...]
    s = cv * _sigmoid(cv)
    o_ref[...] = jnp.dot(s, w_ref[...], precision=HIGHEST, preferred_element_type=F32) + b_ref[...]


def _mod_call(cvec, w_mod, b_mod):
    d, n = w_mod.shape
    tn = 1024
    return pl.pallas_call(
        _mod_kernel,
        grid=(n // tn,),
        in_specs=[pl.BlockSpec((8, d), lambda j: (0, 0)),
                  pl.BlockSpec((d, tn), lambda j: (0, j)),
                  pl.BlockSpec((1, tn), lambda j: (0, j))],
        out_specs=pl.BlockSpec((8, tn), lambda j: (0, j)),
        out_shape=jax.ShapeDtypeStruct((8, n), F32),
        compiler_params=_cparams("arbitrary"),
        name="mod",
    )(cvec, w_mod, b_mod.reshape(1, n))


def _rope(v, cos, sin_a, sin_b, half):
    n = v.shape[-1]
    return v * cos + pltpu.roll(v, n - half, 1) * sin_a + pltpu.roll(v, half, 1) * sin_b


def _in_proj_kernel(x_ref, mod_ref, g1_ref, w_ref, wvt_ref, c64_ref, sa64_ref, sb64_ref,
                    c32_ref, sa32_ref, sb32_ref, qg_ref, kg_ref, seg_ref,
                    zr_ref, zg_ref, zh_ref, zd_ref, vt_ref, gk_ref, gv_ref, kd_ref, vd_ref):
    d = x_ref.shape[-1]
    gw = GROUP_WIDTH
    x = x_ref[...]
    ms = jnp.mean(x * x, axis=-1, keepdims=True)
    h = x * lax.rsqrt(ms + NORM_EPS) * g1_ref[...]
    mod = mod_ref[0]
    h = h * (1.0 + mod[:, d:2 * d]) + mod[:, 0:d]
    hb = h.astype(BF16)

    def proj(c0, c1):
        return jnp.dot(hb, w_ref[:, c0:c1], preferred_element_type=F32)

    c64, sa64, sb64 = c64_ref[...], sa64_ref[...], sb64_ref[...]
    c32, sa32, sb32 = c32_ref[...], sa32_ref[...], sb32_ref[...]
    seg = seg_ref[...]

    zr_ref[:, 0:gw] = _rope(proj(0, gw) * HEAD_DIM ** -0.5, c64, sa64, sb64, 32)
    zr_ref[:, gw:2 * gw] = _rope(proj(gw, 2 * gw), c64, sa64, sb64, 32)
    zr_ref[:, 2 * gw:4 * gw] = proj(2 * gw, 4 * gw)

    o = 4 * gw
    q = proj(o, o + gw)
    qms = jnp.dot(q * q, seg, precision=HIGHEST, preferred_element_type=F32)
    q = q * lax.rsqrt(qms + NORM_EPS) * qg_ref[...]
    q = _rope(q, c64, sa64, sb64, 32) * (HEAD_DIM ** -0.5 * LOG2E)
    lane = lax.broadcasted_iota(jnp.int32, (q.shape[0], 128), 1)
    lo = lane < HEAD_DIM
    qa, qb = q[:, 0:128], q[:, 128:256]
    zg_ref[:, 0:128] = jnp.where(lo, qa, 0.0).astype(BF16)
    zg_ref[:, 128:256] = jnp.where(lo, pltpu.roll(qa, 64, 1), 0.0).astype(BF16)
    zg_ref[:, 256:384] = jnp.where(lo, 0.0, pltpu.roll(qb, 64, 1)).astype(BF16)
    zg_ref[:, 384:512] = jnp.where(lo, 0.0, qb).astype(BF16)
    k = proj(o + gw, o + gw + 128)
    kms = jnp.dot(k * k, seg[0:128, 0:128], precision=HIGHEST, preferred_element_type=F32)
    k = k * lax.rsqrt(kms + NORM_EPS) * kg_ref[...]
    gk_ref[...] = _rope(k, c64[:, 0:128], sa64[:, 0:128], sb64[:, 0:128], 32).astype(BF16)
    v = proj(o + gw + 128, o + gw + 256)
    gv_ref[0] = jnp.where(lo, v, 1.0).astype(BF16)
    gv_ref[1] = jnp.where(lo, 1.0, v).astype(BF16)

    o = 6 * gw
    zh_ref[...] = proj(o, o + 5 * gw)

    o = 11 * gw
    zd_ref[...] = (_rope(proj(o, o + gw), c32, sa32, sb32, 16) * (DIFF_QK_DIM ** -0.5 * LOG2E)).astype(BF16)
    kd_ref[...] = _rope(proj(o + gw, o + 2 * gw), c32, sa32, sb32, 16).astype(BF16)
    v = proj(o + 2 * gw, o + 3 * gw)
    va, vb = v[:, 0:128], v[:, 128:256]
    vd_ref[0] = jnp.where(lo, va, 1.0).astype(BF16)
    vd_ref[1] = jnp.where(lo, pltpu.roll(va, 64, 1), 1.0).astype(BF16)
    vd_ref[2] = jnp.where(lo, vb, 1.0).astype(BF16)
    vd_ref[3] = jnp.where(lo, pltpu.roll(vb, 64, 1), 1.0).astype(BF16)

    vt_ref[...] = lax.dot_general(wvt_ref[...], hb, NT_DIMS, preferred_element_type=F32).astype(BF16)


def _in_proj_call(x_all, mod_all, g1, w_bf, wvt_bf, tabs, qg, kg, seg, n_lat_rows, t_len):
    r, d = x_all.shape
    tm = ROW_TILE
    gw = GROUP_WIDTH
    n_lat_tiles = n_lat_rows // tm
    tiles_per_seq = t_len // tm
    n_batch = n_lat_rows // t_len

    def mod_idx(i):
        return (jnp.where(i < n_lat_tiles, i // tiles_per_seq, n_batch), 0, 0)

    def tab_idx(i):
        return (jnp.where(i < n_lat_tiles, i % tiles_per_seq, tiles_per_seq), 0)

    ctx_tiles = (r - n_lat_rows) // n_batch // tm
    seq_tiles = tiles_per_seq + ctx_tiles

    def kv_idx(i):
        ci = i - n_lat_tiles
        lat = (i // tiles_per_seq) * seq_tiles + ctx_tiles + i % tiles_per_seq
        ctx = (ci // ctx_tiles) * seq_tiles + ci % ctx_tiles
        return (jnp.where(i < n_lat_tiles, lat, ctx), 0)

    row = lambda i: (i, 0)
    const = lambda i: (0, 0)
    tab_spec = pl.BlockSpec((tm, gw), tab_idx)
    return pl.pallas_call(
        _in_proj_kernel,
        grid=(r // tm,),
        in_specs=[pl.BlockSpec((tm, d), row),
                  pl.BlockSpec((1, 1, mod_all.shape[-1]), mod_idx),
                  pl.BlockSpec((1, d), const),
                  pl.BlockSpec(w_bf.shape, const),
                  pl.BlockSpec(wvt_bf.shape, const),
                  tab_spec, tab_spec, tab_spec, tab_spec, tab_spec, tab_spec,
                  pl.BlockSpec((1, gw), const),
                  pl.BlockSpec((1, 128), const),
                  pl.BlockSpec((gw, gw), const)],
        out_specs=[pl.BlockSpec((tm, 4 * gw), row),
                   pl.BlockSpec((tm, 2 * gw), row),
                   pl.BlockSpec((tm, 5 * gw), row),
                   pl.BlockSpec((tm, gw), row),
                   pl.BlockSpec((2 * gw, tm), lambda i: (0, i)),
                   pl.BlockSpec((tm, 128), kv_idx),
                   pl.BlockSpec((GQA_KV_HEADS, tm, 128), lambda i: (0,) + kv_idx(i)),
                   pl.BlockSpec((tm, gw), kv_idx),
                   pl.BlockSpec((GROUP_HEADS, tm, 128), lambda i: (0,) + kv_idx(i))],
        out_shape=[jax.ShapeDtypeStruct((r, 4 * gw), F32),
                   jax.ShapeDtypeStruct((r, 2 * gw), BF16),
                   jax.ShapeDtypeStruct((r, 5 * gw), F32),
                   jax.ShapeDtypeStruct((r, gw), BF16),
                   jax.ShapeDtypeStruct((2 * gw, r), BF16),
                   jax.ShapeDtypeStruct((r, 128), BF16),
                   jax.ShapeDtypeStruct((GQA_KV_HEADS, r, 128), BF16),
                   jax.ShapeDtypeStruct((r, gw), BF16),
                   jax.ShapeDtypeStruct((GROUP_HEADS, r, 128), BF16)],
        compiler_params=_cparams("arbitrary"),
        name="in_proj",
    )(x_all, mod_all, g1, w_bf, wvt_bf, *tabs, qg, kg, seg)


def _rec_direction(q_ref, k_ref, v_ref, vt_ref, par, bd_ref, o_ref, s_ref,
                   lc_ref, qt_ref, kh_ref, kk_ref, tot_ref, a_ref, *, reverse):
    c, gw = q_ref.shape
    sub = REC_SUB
    n_sub = c // sub
    sig = _sigmoid(k_ref[...])
    logf = jnp.log(par + (1.0 - par) * sig)
    kk = (1.0 - par) * (1.0 - sig)
    r_i = lax.broadcasted_iota(jnp.int32, (c, c), 0)
    c_i = lax.broadcasted_iota(jnp.int32, (c, c), 1)
    shift = sub.bit_length() - 1
    same = jnp.right_shift(r_i, shift) == jnp.right_shift(c_i, shift)
    if reverse:
        incl = same & (c_i >= r_i)
        excl = same & (c_i < r_i)
    else:
        incl = same & (c_i <= r_i)
        excl = same & (c_i > r_i)
    lc = jnp.dot(jnp.where(incl, 1.0, 0.0), logf, precision=HIGHEST, preferred_element_type=F32)
    rr = jnp.dot(jnp.where(excl, 1.0, 0.0), logf, precision=HIGHEST, preferred_element_type=F32)
    lc_ref[...] = lc * LOG2E
    qt_ref[...] = (q_ref[...] * jnp.exp(lc)).astype(BF16)
    kh_ref[...] = kk * jnp.exp(rr)
    kk_ref[...] = kk
    tot_ref[...] = lc + rr
    bd = bd_ref[...]
    t_loc = lax.broadcasted_iota(jnp.int32, (sub, gw), 0)
    row_id = lax.broadcasted_iota(jnp.int32, (c, gw), 0)
    half = gw // 2
    head_shift = HEAD_DIM.bit_length() - 1
    same_head = (jnp.right_shift(lax.broadcasted_iota(jnp.int32, (half, half), 0), head_shift)
                 == jnp.right_shift(lax.broadcasted_iota(jnp.int32, (half, half), 1), head_shift))
    quads = (slice(0, half), slice(half, gw))

    def step(j):
        a = (n_sub - 1 - j) if reverse else j
        base = a * sub
        rows = pl.ds(base, sub)
        s_q = [s_ref[qd, qd] for qd in quads]
        qt_a = qt_ref[rows, :]
        o_inter = jnp.concatenate(
            [lax.dot_general(qt_a[:, qd], s.astype(BF16), NT_DIMS, preferred_element_type=F32)
             for qd, s in zip(quads, s_q)], axis=1)
        lc_a = lc_ref[rows, :]
        q_a = q_ref[rows, :]
        k_a = kk_ref[rows, :]
        v_a = v_ref[rows, :]
        for s in range(sub):
            valid = (t_loc <= s) if reverse else (t_loc >= s)
            e = jnp.where(valid, jnp.exp2(lc_a - lc_a[s:s + 1, :]), 0.0)
            a_ref[a, s * sub:(s + 1) * sub, :] = (q_a * e * k_a[s:s + 1, :]).astype(BF16)
        b = jnp.dot(a_ref[a], bd, preferred_element_type=F32)
        o_intra = jnp.zeros((sub, gw), F32)
        for s in range(sub):
            o_intra = o_intra + b[s * sub:(s + 1) * sub, :] * v_a[s:s + 1, :]
        o_ref[rows, :] = o_inter + o_intra
        in_sub = (row_id >= base) & (row_id < base + sub)
        kh_m = jnp.where(in_sub, kh_ref[...], 0.0).astype(BF16)
        decay = jnp.exp(tot_ref[pl.ds(base, 1), :])
        for qd, s in zip(quads, s_q):
            u = jnp.dot(vt_ref[qd, :], kh_m[:, qd], preferred_element_type=F32)
            s_ref[qd, qd] = s * decay[:, qd] + jnp.where(same_head, u, 0.0)

    return step, n_sub


def _rec_kernel(qf_ref, kf_ref, vf_ref, vtf_ref, qb_ref, kb_ref, vb_ref, vtb_ref, par_ref, bd_ref,
                of_ref, ob_ref, sf_ref, sb_ref, *tmp):
    @pl.when(pl.program_id(1) == 0)
    def _():
        sf_ref[...] = jnp.zeros_like(sf_ref)
        sb_ref[...] = jnp.zeros_like(sb_ref)

    n_tmp = len(tmp) // 2
    step_f, n_sub = _rec_direction(qf_ref, kf_ref, vf_ref, vtf_ref, par_ref[0:1, :], bd_ref, of_ref,
                                   sf_ref, *tmp[:n_tmp], reverse=False)
    step_b, _ = _rec_direction(qb_ref, kb_ref, vb_ref, vtb_ref, par_ref[1:2, :], bd_ref, ob_ref,
                               sb_ref, *tmp[n_tmp:], reverse=True)
    for j in range(n_sub):
        step_f(j)
        step_b(j)


def _ret_direction(q_ref, k_ref, v_ref, vt_ref, logg, o_ref, s_ref, *, reverse):
    c, gw = q_ref.shape
    q = q_ref[...]
    k = k_ref[...]
    t_row = lax.broadcasted_iota(jnp.int32, (c, gw), 0).astype(F32)
    if reverse:
        n_q, n_k = c - t_row, t_row
    else:
        n_q, n_k = t_row + 1.0, (c - 1.0) - t_row
    s_t = s_ref[...]
    qt = (q * jnp.exp(n_q * logg)).astype(BF16)
    o = lax.dot_general(qt, s_t.astype(BF16), NT_DIMS, preferred_element_type=F32)
    r_i = lax.broadcasted_iota(jnp.int32, (c, c), 0)
    c_i = lax.broadcasted_iota(jnp.int32, (c, c), 1)
    dist = (c_i - r_i) if reverse else (r_i - c_i)
    allowed = dist >= 0
    dist_f = dist.astype(F32)
    lane = lax.broadcasted_iota(jnp.int32, (c, gw), 1)
    kb = k.astype(BF16)
    vb = v_ref[...].astype(BF16)
    for h in range(GROUP_HEADS):
        in_h = (lane >= h * HEAD_DIM) & (lane < (h + 1) * HEAD_DIM)
        mask = jnp.where(allowed, jnp.exp(dist_f * logg[:, h * HEAD_DIM:h * HEAD_DIM + 1]), 0.0)
        qh = jnp.where(in_h, q, 0.0).astype(BF16)
        sc = lax.dot_general(qh, kb, NT_DIMS, preferred_element_type=F32) * mask
        oh = jnp.dot(sc.astype(BF16), vb, preferred_element_type=F32)
        o = o + jnp.where(in_h, oh, 0.0)
    o_ref[...] = o
    kh = (k * jnp.exp(n_k * logg)).astype(BF16)
    u = jnp.dot(vt_ref[...], kh, preferred_element_type=F32)
    head_shift = HEAD_DIM.bit_length() - 1
    same_head = (jnp.right_shift(lax.broadcasted_iota(jnp.int32, (gw, gw), 0), head_shift)
                 == jnp.right_shift(lax.broadcasted_iota(jnp.int32, (gw, gw), 1), head_shift))
    s_ref[...] = s_t * jnp.exp(float(c) * logg) + jnp.where(same_head, u, 0.0)


def _ret_kernel(qf_ref, kf_ref, vf_ref, vtf_ref, qb_ref, kb_ref, vb_ref, vtb_ref, par_ref, bd_ref,
                of_ref, ob_ref, sf_ref, sb_ref):
    @pl.when(pl.program_id(1) == 0)
    def _():
        sf_ref[...] = jnp.zeros_like(sf_ref)
        sb_ref[...] = jnp.zeros_like(sb_ref)

    _ret_direction(qf_ref, kf_ref, vf_ref, vtf_ref, par_ref[0:1, :], of_ref, sf_ref, reverse=False)
    _ret_direction(qb_ref, kb_ref, vb_ref, vtb_ref, par_ref[1:2, :], ob_ref, sb_ref, reverse=True)


def _rec_call(z, vt_all, par, bd, *, hgrn, n_batch, t_len, ctx_len, cols, vt_row):
    r = z.shape[0]
    c = REC_CHUNK
    gw = GROUP_WIDTH
    n_ctx = ctx_len // c
    n_lat = t_len // c
    lat_blocks = n_batch * n_lat

    def fwd_blk(b, i):
        return jnp.where(i < n_ctx, lat_blocks + b * n_ctx + i, b * n_lat + i - n_ctx)

    def bwd_blk(b, i):
        return jnp.where(i < n_ctx, lat_blocks + b * n_ctx + (n_ctx - 1 - i),
                         b * n_lat + (n_lat - 1 - (i - n_ctx)))

    def zspec(blk, col):
        return pl.BlockSpec((c, gw), lambda b, i: (blk(b, i), col))

    def vtspec(blk):
        return pl.BlockSpec((gw, c), lambda b, i: (vt_row, blk(b, i)))

    const = lambda b, i: (0, 0)
    cq, ckf, ckb, cv = cols
    state_scratch = [pltpu.VMEM((gw, gw), F32), pltpu.VMEM((gw, gw), F32)]
    if hgrn:
        body = _rec_kernel
        scratch = state_scratch + 2 * [
            pltpu.VMEM((c, gw), F32), pltpu.VMEM((c, gw), BF16), pltpu.VMEM((c, gw), F32),
            pltpu.VMEM((c, gw), F32), pltpu.VMEM((c, gw), F32),
            pltpu.VMEM((c // REC_SUB, REC_SUB * REC_SUB, gw), BF16)]
    else:
        body = _ret_kernel
        scratch = state_scratch
    out_f, out_b = pl.pallas_call(
        body,
        grid=(n_batch, n_ctx + n_lat),
        in_specs=[zspec(fwd_blk, cq), zspec(fwd_blk, ckf), zspec(fwd_blk, cv), vtspec(fwd_blk),
                  zspec(bwd_blk, cq), zspec(bwd_blk, ckb), zspec(bwd_blk, cv), vtspec(bwd_blk),
                  pl.BlockSpec((2, gw), const),
                  pl.BlockSpec((gw, gw), const)],
        out_specs=[pl.BlockSpec((c, gw), lambda b, i: (fwd_blk(b, i), 0)),
                   pl.BlockSpec((c, gw), lambda b, i: (bwd_blk(b, i), 0))],
        out_shape=[jax.ShapeDtypeStruct((r, gw), F32), jax.ShapeDtypeStruct((r, gw), F32)],
        scratch_shapes=scratch,
        compiler_params=_cparams("arbitrary", "arbitrary"),
        name="hgrn_rec" if hgrn else "ret_rec",
    )(z, z, z, vt_all, z, z, z, vt_all, par, bd)
    return out_f, out_b


def _online_softmax(lhs_ref, k_ref, v_ref, kv_chunk, scr):
    m_ref, acc_ref = scr[0:2]
    s_refs, p_refs, a_refs, x_refs = scr[2:4], scr[4:6], scr[6:8], scr[8:10]
    n_heads = lhs_ref.shape[0]
    n = k_ref.shape[0] // kv_chunk
    steps = n_heads * n
    assert steps % 2 == 0 and steps >= 4
    m_ref[...] = jnp.full(m_ref.shape, -jnp.inf, F32)
    acc_ref[...] = jnp.zeros(acc_ref.shape, F32)

    def split(t):
        if isinstance(t, int):
            return t // n, pl.ds((t % n) * kv_chunk, kv_chunk)
        h = t // n
        return h, pl.ds(pl.multiple_of((t - h * n) * kv_chunk, kv_chunk), kv_chunk)

    def qk(t, slot):
        h, rows = split(t)
        s = lax.dot_general(lhs_ref[h], k_ref[rows, :], NT_DIMS, preferred_element_type=F32)
        s_refs[slot][...] = s
        x_refs[slot][...] = jnp.max(s, axis=-1, keepdims=True)

    def sm(t, slot):
        h, _ = split(t)
        s = s_refs[slot][...]
        m = m_ref[h]
        m_new = jnp.maximum(m, x_refs[slot][...])
        m_ref[h] = m_new
        a_refs[slot][...] = jnp.exp2(m - m_new)
        p_refs[slot][...] = jnp.exp2((s - m_new).astype(BF16))

    def pv(t, slot):
        h, rows = split(t)
        v_blk = v_ref[rows, :] if len(v_ref.shape) == 2 else v_ref[h, rows, :]
        acc_ref[h] = a_refs[slot][...] * acc_ref[h] + jnp.dot(
            p_refs[slot][...], v_blk, preferred_element_type=F32)

    qk(0, 0)
    qk(1, 1)
    sm(0, 0)

    def body(i, carry):
        t = 2 * i + 1
        qk(t + 1, 0)
        sm(t, 1)
        pv(t - 1, 0)
        qk(t + 2, 1)
        sm(t + 1, 0)
        pv(t, 1)
        return carry

    lax.fori_loop(0, (steps - 2) // 2, body, 0)
    sm(steps - 1, 1)
    pv(steps - 2, 0)
    pv(steps - 1, 1)


def _gqa_kernel(q_ref, k_ref, v_ref, o_ref, lhs_ref, *scr, kv_chunk):
    tq = q_ref.shape[0]
    lane = lax.broadcasted_iota(jnp.int32, (tq, 128), 1)
    lo = lane < HEAD_DIM
    for j in range(GQA_KV_HEADS):
        lhs_ref[j, 0:tq, :] = q_ref[:, (2 * j) * 128:(2 * j + 1) * 128]
        lhs_ref[j, tq:2 * tq, :] = q_ref[:, (2 * j + 1) * 128:(2 * j + 2) * 128]
    _online_softmax(lhs_ref, k_ref, v_ref, kv_chunk, scr)
    acc_ref = scr[1]
    lo2 = lax.broadcasted_iota(jnp.int32, (2 * tq, 128), 1) < HEAD_DIM
    acc0, acc1 = acc_ref[0], acc_ref[1]
    o0 = acc0 / jnp.where(lo2, pltpu.roll(acc0, 64, 1), 1.0)
    o1 = acc1 / jnp.where(lo2, 1.0, pltpu.roll(acc1, 64, 1))
    o_ref[:, 0:128] = jnp.where(lo, o0[0:tq], pltpu.roll(o0[tq:2 * tq], 64, 1))
    o_ref[:, 128:256] = jnp.where(lo, pltpu.roll(o1[0:tq], 64, 1), o1[tq:2 * tq])


def _diff_kernel(lam_ref, q_ref, k_ref, v_ref, ng_ref, seg_ref, o_ref, lhs_ref, *scr, kv_chunk,
                 out_scale):
    tq, gw = q_ref.shape
    lane = lax.broadcasted_iota(jnp.int32, (tq, gw), 1)
    q = q_ref[...].astype(F32)
    lam = lam_ref[0, 0]
    for h in range(GROUP_HEADS):
        base = h * HEAD_DIM
        q1 = jnp.where((lane >= base) & (lane < base + DIFF_QK_DIM), q, 0.0)
        q2 = jnp.where((lane >= base + DIFF_QK_DIM) & (lane < base + HEAD_DIM), q, 0.0)
        lhs_ref[h, 0:tq, :] = q1.astype(BF16)
        lhs_ref[h, tq:2 * tq, :] = q2.astype(BF16)
    _online_softmax(lhs_ref, k_ref, v_ref, kv_chunk, scr)
    acc_ref = scr[1]
    lo = lax.broadcasted_iota(jnp.int32, (tq, 128), 1) < HEAD_DIM
    lo2 = lax.broadcasted_iota(jnp.int32, (2 * tq, 128), 1) < HEAD_DIM
    heads = []
    for h in range(GROUP_HEADS):
        acc = acc_ref[h]
        o = acc / jnp.where(lo2, pltpu.roll(acc, 64, 1), 1.0)
        heads.append(o[0:tq] - lam * o[tq:2 * tq])
    out = jnp.concatenate([jnp.where(lo, heads[0], pltpu.roll(heads[1], 64, 1)),
                           jnp.where(lo, heads[2], pltpu.roll(heads[3], 64, 1))], axis=1)
    ms = jnp.dot(out * out, seg_ref[...], precision=HIGHEST, preferred_element_type=F32)
    o_ref[...] = out * lax.rsqrt(ms + NORM_EPS) * ng_ref[...] * out_scale


def _kv_chunk(s_len, n_heads):
    best = None
    for kc in range(128, min(s_len, ATT_KV_CHUNK_MAX) + 1, 128):
        if s_len % kc == 0 and n_heads * (s_len // kc) >= 4:
            best = kc
    assert best is not None, s_len
    return best


def _attn_call(kind, zq, kv, *, n_batch, t_len, ctx_len, ctx_queries, lam=None, norm_g=None,
               seg=None, out_scale=None):
    gw = GROUP_WIDTH
    lat_rows = n_batch * t_len
    s_all = ctx_len + t_len
    if kind == "gqa":
        qw, kw, n_heads = 512, 128, GQA_KV_HEADS
    else:
        qw, kw, n_heads = 256, 256, GROUP_HEADS
    if ctx_queries:
        tq, s_len = ctx_len, ctx_len
        grid = (n_batch, 1)
        q_map = lambda b, i: (lat_rows // ctx_len + b, 0)
        out_map = lambda b, i: (b, 0)
        kv_blk = lambda b: b * (s_all // ctx_len)
    else:
        tq, s_len = ATT_Q_TILE[kind], s_all
        grid = (n_batch, t_len // tq)
        q_map = lambda b, i: (b * (t_len // tq) + i, 0)
        out_map = q_map
        kv_blk = lambda b: b
    kc = _kv_chunk(s_len, n_heads)
    nv = 128
    kv_specs = [pl.BlockSpec((s_len, kw), lambda b, i: (kv_blk(b), 0)),
                pl.BlockSpec((n_heads, s_len, nv), lambda b, i: (0, kv_blk(b), 0))]
    kv_args = kv
    out_spec = pl.BlockSpec((tq, gw), out_map)
    out_shape = jax.ShapeDtypeStruct((n_batch * tq * grid[1], gw), F32)
    q_spec = pl.BlockSpec((tq, qw), q_map)
    const = lambda b, i: (0, 0)
    m_rows = 2 * tq
    scratch = [pltpu.VMEM((n_heads, m_rows, kw), BF16),
               pltpu.VMEM((n_heads, m_rows, 1), F32),
               pltpu.VMEM((n_heads, m_rows, nv), F32),
               pltpu.VMEM((m_rows, kc), F32), pltpu.VMEM((m_rows, kc), F32),
               pltpu.VMEM((m_rows, kc), BF16), pltpu.VMEM((m_rows, kc), BF16),
               pltpu.VMEM((m_rows, 1), F32), pltpu.VMEM((m_rows, 1), F32),
               pltpu.VMEM((m_rows, 1), F32), pltpu.VMEM((m_rows, 1), F32)]
    if kind == "gqa":
        return pl.pallas_call(
            functools.partial(_gqa_kernel, kv_chunk=kc),
            grid=grid, in_specs=[q_spec] + kv_specs, out_specs=out_spec, out_shape=out_shape,
            scratch_shapes=scratch,
            compiler_params=_cparams("arbitrary", "arbitrary"),
            name="gqa_ctx" if ctx_queries else "gqa_lat",
        )(zq, *kv_args)
    return pl.pallas_call(
        functools.partial(_diff_kernel, kv_chunk=kc, out_scale=out_scale),
        grid=grid,
        in_specs=[pl.BlockSpec(memory_space=pltpu.SMEM), q_spec] + kv_specs
        + [pl.BlockSpec((1, gw), const), pl.BlockSpec((gw, gw), const)],
        out_specs=out_spec, out_shape=out_shape,
        scratch_shapes=scratch,
        compiler_params=_cparams("arbitrary", "arbitrary"),
        name="diff_ctx" if ctx_queries else "diff_lat",
    )(lam, zq, *kv_args, norm_g, seg)


def _out_proj_kernel(rf_ref, rb_ref, rg_ref, ga_ref, hf_ref, hb_ref, hg_ref, df_ref, x_ref, mod_ref,
                     w_ref, rn_ref, hn_ref, seg_ref, n2_ref, rw_ref, rbias_ref,
                     xo_ref, h2_ref, e_ref, gt_ref):
    sub = OUT_SUB_TILE
    for i0 in range(0, x_ref.shape[0], sub):
        _out_proj_rows(slice(i0, i0 + sub), rf_ref, rb_ref, rg_ref, ga_ref, hf_ref, hb_ref, hg_ref,
                       df_ref, x_ref, mod_ref, w_ref, rn_ref, hn_ref, seg_ref, n2_ref, rw_ref,
                       rbias_ref, xo_ref, h2_ref, e_ref, gt_ref)


def _out_proj_rows(rs, rf_ref, rb_ref, rg_ref, ga_ref, hf_ref, hb_ref, hg_ref, df_ref, x_ref, mod_ref,
                   w_ref, rn_ref, hn_ref, seg_ref, n2_ref, rw_ref, rbias_ref,
                   xo_ref, h2_ref, e_ref, gt_ref):
    d = x_ref.shape[-1]
    gw = GROUP_WIDTH
    seg = seg_ref[...]
    mod = mod_ref[0]

    def readout(o, norm_g, gate):
        ms = jnp.dot(o * o, seg, precision=HIGHEST, preferred_element_type=F32)
        return o * lax.rsqrt(ms + NORM_EPS) * norm_g * (gate * _sigmoid(gate))

    parts = (readout(rf_ref[rs, :] + rb_ref[rs, :], rn_ref[...], rg_ref[rs, :]),
             ga_ref[rs, :],
             readout(hf_ref[rs, :] + hb_ref[rs, :], hn_ref[...], hg_ref[rs, :]),
             df_ref[rs, :])
    acc = jnp.zeros((rs.stop - rs.start, d), F32)
    for n, part in enumerate(parts):
        acc = acc + jnp.dot(part.astype(BF16), w_ref[n * gw:(n + 1) * gw, :],
                            preferred_element_type=F32)
    x = x_ref[rs, :] + mod[:, 2 * d:3 * d] * acc
    xo_ref[rs, :] = x
    ms = jnp.mean(x * x, axis=-1, keepdims=True)
    h2 = x * lax.rsqrt(ms + NORM_EPS) * n2_ref[...]
    h2 = h2 * (1.0 + mod[:, 4 * d:5 * d]) + mod[:, 3 * d:4 * d]
    h2_ref[rs, :] = h2

    logits = lax.dot_general(rw_ref[...], h2, NT_DIMS, precision=HIGHEST, preferred_element_type=F32)
    score = _sigmoid(logits)
    sel = score + rbias_ref[:, 0:1]
    ng = N_EXPERT_GROUPS
    sc = [score[m * ng:(m + 1) * ng, :] for m in range(EXPERTS_PER_GROUP)]
    sl = [sel[m * ng:(m + 1) * ng, :] for m in range(EXPERTS_PER_GROUP)]
    hi1, lo1 = jnp.maximum(sl[0], sl[1]), jnp.minimum(sl[0], sl[1])
    hi2, lo2 = jnp.maximum(sl[2], sl[3]), jnp.minimum(sl[2], sl[3])
    group_score = jnp.maximum(hi1, hi2) + jnp.maximum(jnp.minimum(hi1, hi2), jnp.maximum(lo1, lo2))
    g_iota = lax.broadcasted_iota(jnp.int32, group_score.shape, 0).astype(F32)
    g_max = jnp.max(group_score, axis=0, keepdims=True)
    best = jnp.min(jnp.where(group_score == g_max, g_iota, float(ng)), axis=0, keepdims=True)
    pick = g_iota == best
    v = [jnp.sum(jnp.where(pick, a, 0.0), axis=0, keepdims=True) for a in sl]
    u = [jnp.sum(jnp.where(pick, a, 0.0), axis=0, keepdims=True) for a in sc]
    rank = []
    for i in range(EXPERTS_PER_GROUP):
        r_i = jnp.zeros(best.shape, F32)
        for j in range(EXPERTS_PER_GROUP):
            if j < i:
                r_i = r_i + jnp.where(v[j] >= v[i], 1.0, 0.0)
            elif j > i:
                r_i = r_i + jnp.where(v[j] > v[i], 1.0, 0.0)
        rank.append(r_i)
    idx, gate = [], []
    for kth in range(2):
        idx.append(sum(jnp.where(rank[i] == kth, float(i), 0.0) for i in range(EXPERTS_PER_GROUP)))
        gate.append(sum(jnp.where(rank[i] == kth, u[i], 0.0) for i in range(EXPERTS_PER_GROUP)))
    tot = gate[0] + gate[1]
    e_ref[0:1, rs] = (best * EXPERTS_PER_GROUP + idx[0]).astype(jnp.int32)
    e_ref[1:2, rs] = (best * EXPERTS_PER_GROUP + idx[1]).astype(jnp.int32)
    gt_ref[0:1, rs] = gate[0] / tot
    gt_ref[1:2, rs] = gate[1] / tot


def _out_proj_call(rf, rb, zr, o_gqa, hf, hb, zh, o_df, x_all, mod_all, w_out_bf, rn, hn, seg, n2,
                   rw_t, rbias, n_rows, n_lat_rows, t_len):
    d = x_all.shape[-1]
    tm = OUT_TILE
    gw = GROUP_WIDTH
    n_lat_tiles = n_lat_rows // tm
    tiles_per_seq = t_len // tm
    n_batch = n_lat_rows // t_len

    def mod_idx(i):
        return (jnp.where(i < n_lat_tiles, i // tiles_per_seq, n_batch), 0, 0)

    row = lambda i: (i, 0)
    const = lambda i: (0, 0)
    blk = lambda col: pl.BlockSpec((tm, gw), lambda i: (i, col))
    return pl.pallas_call(
        _out_proj_kernel,
        grid=(n_rows // tm,),
        in_specs=[blk(0), blk(0), blk(3), blk(0), blk(0), blk(0), blk(4), blk(0),
                  pl.BlockSpec((tm, d), row),
                  pl.BlockSpec((1, 1, mod_all.shape[-1]), mod_idx),
                  pl.BlockSpec((4 * gw, d), const),
                  pl.BlockSpec((1, gw), const), pl.BlockSpec((1, gw), const),
                  pl.BlockSpec((gw, gw), const),
                  pl.BlockSpec((1, d), const),
                  pl.BlockSpec((N_EXPERTS, d), const),
                  pl.BlockSpec((N_EXPERTS, 128), const)],
        out_specs=[pl.BlockSpec((tm, d), row), pl.BlockSpec((tm, d), row),
                   pl.BlockSpec((2, tm), lambda i: (0, i)), pl.BlockSpec((2, tm), lambda i: (0, i))],
        out_shape=[jax.ShapeDtypeStruct((n_rows, d), F32), jax.ShapeDtypeStruct((n_rows, d), F32),
                   jax.ShapeDtypeStruct((2, n_rows), jnp.int32),
                   jax.ShapeDtypeStruct((2, n_rows), F32)],
        compiler_params=_cparams("arbitrary"),
        name="out_proj",
    )(rf, rb, zr, o_gqa, hf, hb, zh, o_df, x_all, mod_all, w_out_bf, rn, hn, seg, n2, rw_t, rbias)


def _expert_kernel(be_ref, nb_ref, idx_hbm, gate_ref, h2_hbm, wg_ref, wu_ref, wd_ref, ys_hbm,
                   idx_ref, x_ref, y_ref, wgb_ref, wub_ref, wdb_ref, sem_i, sem_g, sem_s):
    i = pl.program_id(0)
    nb = nb_ref[0]
    rows = MOE_BLOCK

    def idx_copy(blk):
        return pltpu.make_async_copy(idx_hbm.at[blk], idx_ref.at[blk % 3], sem_i.at[blk % 3])

    def issue_gather(blk, xslot):
        islot = blk % 3
        for r in range(rows):
            pltpu.make_async_copy(h2_hbm.at[pl.ds(idx_ref[islot, r], 1)],
                                  x_ref.at[xslot, pl.ds(r, 1)], sem_g.at[xslot]).start()

    def wait_slot(buf_ref, sem, slot):
        pltpu.make_async_copy(buf_ref.at[slot], buf_ref.at[slot], sem.at[slot]).wait()

    def step(slot):
        @pl.when(i == 0)
        def _():
            y_ref[...] = jnp.zeros(y_ref.shape, F32)
            n_real = ys_hbm.shape[0] - 2 * rows
            for s in range(2):
                fill = pltpu.make_async_copy(y_ref.at[s], ys_hbm.at[pl.ds(n_real + s * rows, rows)],
                                             sem_s.at[s])
                fill.start()
                fill.wait()
            idx_copy(0).start()
            idx_copy(0).wait()
            issue_gather(0, 0)

            @pl.when(nb > 1)
            def _():
                idx_copy(1).start()

        @pl.when(i + 1 < nb)
        def _():
            idx_copy(i + 1).wait()
            issue_gather(i + 1, 1 - slot)

            @pl.when(i + 2 < nb)
            def _():
                idx_copy(i + 2).start()

        wait_slot(x_ref, sem_g, slot)

        @pl.when(i >= 2)
        def _():
            wait_slot(y_ref, sem_s, slot)

        @pl.when((i == 0) | (be_ref[i] != be_ref[jnp.maximum(i - 1, 0)]))
        def _():
            wgb_ref[...] = wg_ref[0].astype(BF16)
            wub_ref[...] = wu_ref[0].astype(BF16)
            wdb_ref[...] = wd_ref[0].astype(BF16)

        xb = x_ref[slot].astype(BF16)
        a = jnp.dot(xb, wgb_ref[...], preferred_element_type=F32)
        u = jnp.dot(xb, wub_ref[...], preferred_element_type=F32)
        hmid = (a * _sigmoid(a) * u).astype(BF16)
        y_ref[slot] = jnp.dot(hmid, wdb_ref[...], preferred_element_type=F32) * gate_ref[:, 0:1]

        islot = i % 3
        for r in range(rows):
            pltpu.make_async_copy(y_ref.at[slot, pl.ds(r, 1)],
                                  ys_hbm.at[pl.ds(idx_ref[islot, rows + r], 1)], sem_s.at[slot]).start()

        @pl.when(i == nb - 1)
        def _():
            @pl.when(i >= 1)
            def _():
                wait_slot(y_ref, sem_s, 1 - slot)
            wait_slot(y_ref, sem_s, slot)

    @pl.when(i < nb)
    def _():
        for parity in range(2):
            @pl.when(i % 2 == parity)
            def _(parity=parity):
                step(parity)


def _expert_call(block_expert, n_used, row_idx, row_gate, h2, wg, wu, wd, n_tok, layer):
    n_blocks = block_expert.shape[0]
    d = h2.shape[-1]
    ff = wg.shape[-1]
    rows = MOE_BLOCK
    grid_spec = pltpu.PrefetchScalarGridSpec(
        num_scalar_prefetch=2,
        grid=(n_blocks,),
        in_specs=[pl.BlockSpec(memory_space=pl.ANY),
                  pl.BlockSpec((rows, 1), lambda i, be, nb: (i, 0)),
                  pl.BlockSpec(memory_space=pl.ANY),
                  pl.BlockSpec((None, 1, d, ff), lambda i, be, nb: (layer, be[i], 0, 0)),
                  pl.BlockSpec((None, 1, d, ff), lambda i, be, nb: (layer, be[i], 0, 0)),
                  pl.BlockSpec((None, 1, ff, d), lambda i, be, nb: (layer, be[i], 0, 0))],
        out_specs=pl.BlockSpec(memory_space=pl.ANY),
        scratch_shapes=[pltpu.SMEM((3, 2 * rows), jnp.int32),
                        pltpu.VMEM((2, rows, d), F32),
                        pltpu.VMEM((2, rows, d), F32),
                        pltpu.VMEM((d, ff), BF16), pltpu.VMEM((d, ff), BF16), pltpu.VMEM((ff, d), BF16),
                        pltpu.SemaphoreType.DMA((3,)), pltpu.SemaphoreType.DMA((2,)),
                        pltpu.SemaphoreType.DMA((2,))],
    )
    return pl.pallas_call(
        _expert_kernel,
        grid_spec=grid_spec,
        out_shape=jax.ShapeDtypeStruct((2 * n_tok + 2 * rows, d), F32),
        compiler_params=_cparams("arbitrary"),
        name="experts",
    )(block_expert, n_used, row_idx, row_gate, h2, wg, wu, wd)


def _dispatch_plan(e_t, g_t, n_tok):
    m = 2 * n_tok
    e_flat = e_t.reshape(m)
    order = jnp.argsort(e_flat).astype(jnp.int32)
    experts = jnp.arange(N_EXPERTS, dtype=jnp.int32)
    counts = jnp.sum((e_flat[:, None] == experts[None, :]).astype(jnp.int32), axis=0)
    padded = (counts + MOE_BLOCK - 1) // MOE_BLOCK * MOE_BLOCK
    start = jnp.cumsum(counts) - counts
    pend = jnp.cumsum(padded)
    pstart = pend - padded
    n_blocks = -(-m // MOE_BLOCK) + N_EXPERTS
    blk_row0 = jnp.arange(n_blocks, dtype=jnp.int32) * MOE_BLOCK
    block_expert = jnp.minimum(
        jnp.sum((pend[None, :] <= blk_row0[:, None]).astype(jnp.int32), axis=1), N_EXPERTS - 1)
    blk = jnp.arange(n_blocks, dtype=jnp.int32)[:, None]
    r_in = jnp.arange(MOE_BLOCK, dtype=jnp.int32)[None, :]
    pos = blk_row0[:, None] + r_in - pstart[block_expert][:, None]
    valid = pos < counts[block_expert][:, None]
    sorted_pos = jnp.clip(start[block_expert][:, None] + pos, 0, m - 1)
    assign = order[sorted_pos]
    dump = m + (blk % 2) * MOE_BLOCK + r_in
    row_src = jnp.where(valid, assign % n_tok, 0)
    row_dst = jnp.where(valid, assign, dump)
    row_idx = jnp.concatenate([row_src, row_dst], axis=1).astype(jnp.int32)
    row_gate = jnp.where(valid, g_t.reshape(m)[assign], 0.0)
    n_used = (pend[-1] // MOE_BLOCK).astype(jnp.int32).reshape(1)
    return block_expert.astype(jnp.int32), n_used, row_idx, row_gate.reshape(n_blocks * MOE_BLOCK, 1)


def _combine_kernel(x_ref, y0_ref, y1_ref, mod_ref, fg_ref, o_ref, *, final_norm):
    d = x_ref.shape[-1]
    mod = mod_ref[0]
    x = x_ref[...] + mod[:, 5 * d:6 * d] * (y0_ref[...] + y1_ref[...])
    if final_norm:
        ms = jnp.mean(x * x, axis=-1, keepdims=True)
        x = x * lax.rsqrt(ms + NORM_EPS) * fg_ref[...]
    o_ref[...] = x


def _combine_call(x_new, ys, mod_all, final_g, n_rows, n_lat_rows, t_len, final_norm):
    d = x_new.shape[-1]
    tm = ROW_TILE
    n_lat_tiles = n_lat_rows // tm
    tiles_per_seq = t_len // tm
    n_batch = n_lat_rows // t_len
    n_tiles = n_rows // tm

    def mod_idx(i):
        return (jnp.where(i < n_lat_tiles, i // tiles_per_seq, n_batch), 0, 0)

    return pl.pallas_call(
        functools.partial(_combine_kernel, final_norm=final_norm),
        grid=(n_tiles,),
        in_specs=[pl.BlockSpec((tm, d), lambda i: (i, 0)),
                  pl.BlockSpec((tm, d), lambda i: (i, 0)),
                  pl.BlockSpec((tm, d), lambda i: (i + n_tiles, 0)),
                  pl.BlockSpec((1, 1, mod_all.shape[-1]), mod_idx),
                  pl.BlockSpec((1, d), lambda i: (0, 0))],
        out_specs=pl.BlockSpec((tm, d), lambda i: (i, 0)),
        out_shape=jax.ShapeDtypeStruct((n_rows, d), F32),
        compiler_params=_cparams("arbitrary"),
        name="combine",
    )(x_new, ys, ys, mod_all, final_g)


def _rope_tables(t_len, rot_dim, pad_rows):
    n_freq = rot_dim // 4
    half = rot_dim // 2
    inv_freq = ROPE_BASE ** (-jnp.arange(n_freq, dtype=F32) / n_freq)
    pos = jnp.arange(t_len)
    row = (pos // GRID_W).astype(F32)
    col = (pos % GRID_W).astype(F32)
    ang = jnp.concatenate([row[:, None] * inv_freq, col[:, None] * inv_freq], axis=-1)
    cos, sin = jnp.cos(ang), jnp.sin(ang)
    reps = GROUP_WIDTH // rot_dim
    zeros = jnp.zeros_like(sin)
    cos_f = jnp.tile(jnp.concatenate([cos, cos], -1), (1, reps))
    sin_a = jnp.tile(jnp.concatenate([-sin, zeros], -1), (1, reps))
    sin_b = jnp.tile(jnp.concatenate([zeros, sin], -1), (1, reps))
    pad1 = jnp.ones((pad_rows, GROUP_WIDTH), F32)
    pad0 = jnp.zeros((pad_rows, GROUP_WIDTH), F32)
    return (jnp.concatenate([cos_f, pad1], 0), jnp.concatenate([sin_a, pad0], 0),
            jnp.concatenate([sin_b, pad0], 0))


def kernel(x, c, ctx, c_ctx, w_mod, b_mod, norm1_g, norm2_g, w_in, ret_decay_logit, ret_norm_g,
           gqa_qnorm_g, gqa_knorm_g, hgrn_lb_logit, hgrn_norm_g, diff_lambda, diff_norm_g, w_out,
           router_w, router_bias, moe_w_gate, moe_w_up, moe_w_down, final_norm_g):
    n_batch, t_len, d = x.shape
    ctx_len = ctx.shape[1]
    depth = w_mod.shape[0]
    gw = GROUP_WIDTH
    lat_rows = n_batch * t_len
    all_rows = lat_rows + n_batch * ctx_len
    assert t_len % ROW_TILE == 0 and ctx_len % ROW_TILE == 0
    assert all(t_len % tq == 0 for tq in ATT_Q_TILE.values())
    assert t_len % OUT_TILE == 0 and (n_batch * ctx_len) % OUT_TILE == 0
    assert ctx_len % REC_CHUNK == 0 and (ctx_len + t_len) % ctx_len == 0 and n_batch + 1 <= 8

    x_all = jnp.concatenate([x.reshape(lat_rows, d), ctx.reshape(n_batch * ctx_len, d)], axis=0)
    cvec = jnp.zeros((8, d), F32).at[:n_batch].set(c).at[n_batch].set(c_ctx)

    tabs = _rope_tables(t_len, HEAD_DIM, ROW_TILE) + _rope_tables(t_len, DIFF_QK_DIM, ROW_TILE)
    head_of = jnp.arange(gw) // HEAD_DIM
    same_head = head_of[:, None] == head_of[None, :]
    seg = same_head.astype(F32) / HEAD_DIM
    bd = same_head.astype(BF16)

    sm = jax.nn.softmax(hgrn_lb_logit.astype(F32), axis=1)
    lower_bounds = jnp.cumsum(sm, axis=1) - sm[:, :1]
    log_gamma = jax.nn.log_sigmoid(ret_decay_logit.astype(F32))

    perm = (jnp.arange(N_EXPERTS) % N_EXPERT_GROUPS) * EXPERTS_PER_GROUP + jnp.arange(N_EXPERTS) // N_EXPERT_GROUPS
    rw_t = router_w.T[perm]
    rbias = jnp.broadcast_to(router_bias.astype(F32)[perm][:, None], (N_EXPERTS, 128))

    tile = lambda g: jnp.tile(g.astype(F32), GROUP_HEADS).reshape(1, gw)

    out = None
    for layer in range(depth):
        need_ctx = layer < depth - 1
        lambda_init = 0.8 - 0.6 * math.exp(-0.3 * layer)
        mod_all = _mod_call(cvec, w_mod[layer], b_mod[layer]).reshape(8, 1, 6 * d)

        w_l = w_in[layer]
        w_bf = w_l.astype(BF16)
        wvt_bf = jnp.concatenate([w_l[:, 2 * gw:3 * gw], w_l[:, 9 * gw:10 * gw]], axis=1).T.astype(BF16)
        zr, zg, zh, zd, vt_all, kg, vg, kd, vd = _in_proj_call(
            x_all, mod_all, norm1_g[layer].reshape(1, d), w_bf, wvt_bf, tabs,
            tile(gqa_qnorm_g[layer]), tile(gqa_knorm_g[layer])[:, :128], seg, lat_rows, t_len)

        rec_kw = dict(n_batch=n_batch, t_len=t_len, ctx_len=ctx_len)
        ret_par = jnp.repeat(log_gamma[layer], HEAD_DIM, axis=-1)
        rf, rb = _rec_call(zr, vt_all, ret_par, bd, hgrn=False, cols=(0, 1, 1, 2), vt_row=0, **rec_kw)
        hf, hb = _rec_call(zh, vt_all, lower_bounds[:, layer], bd, hgrn=True, cols=(0, 1, 2, 3),
                           vt_row=1, **rec_kw)

        lp = diff_lambda[layer].astype(F32)
        lam = (jnp.exp(jnp.sum(lp[0] * lp[1])) - jnp.exp(jnp.sum(lp[2] * lp[3])) + lambda_init)
        lam = lam.reshape(1, 1)
        att_kw = dict(n_batch=n_batch, t_len=t_len, ctx_len=ctx_len)
        diff_kw = dict(lam=lam, norm_g=tile(diff_norm_g[layer]), seg=seg, out_scale=1.0 - lambda_init)
        o_gqa = _attn_call("gqa", zg, (kg, vg), ctx_queries=False, **att_kw)
        o_df = _attn_call("diff", zd, (kd, vd), ctx_queries=False, **att_kw, **diff_kw)
        n_rows = lat_rows
        if need_ctx:
            o_gqa = jnp.concatenate(
                [o_gqa, _attn_call("gqa", zg, (kg, vg), ctx_queries=True, **att_kw)], 0)
            o_df = jnp.concatenate(
                [o_df, _attn_call("diff", zd, (kd, vd), ctx_queries=True, **att_kw, **diff_kw)], 0)
            n_rows = all_rows

        x_new, h2, e_t, g_t = _out_proj_call(
            rf, rb, zr, o_gqa, hf, hb, zh, o_df, x_all, mod_all, w_out[layer].astype(BF16),
            tile(ret_norm_g[layer]), tile(hgrn_norm_g[layer]), seg, norm2_g[layer].reshape(1, d),
            rw_t, rbias, n_rows, lat_rows, t_len)

        plan = _dispatch_plan(e_t, g_t, n_rows)
        ys = _expert_call(*plan, h2, moe_w_gate, moe_w_up, moe_w_down, n_rows, layer)
        last = layer == depth - 1
        x_next = _combine_call(x_new, ys, mod_all, final_norm_g.reshape(1, d), n_rows, lat_rows, t_len,
                               final_norm=last)
        if last:
            out = x_next[:lat_rows].reshape(n_batch, t_len, d)
        else:
            x_all = x_next
    return out
```

```python
import functools
import math

import jax
import jax.numpy as jnp
from jax import lax
from jax.experimental import pallas as pl
from jax.experimental.pallas import tpu as pltpu

F32 = jnp.float32
BF16 = jnp.bfloat16
HIGHEST = lax.Precision.HIGHEST

HEAD_DIM = 64
GROUP_HEADS = 4
GROUP_WIDTH = GROUP_HEADS * HEAD_DIM
GQA_KV_HEADS = 2
DIFF_QK_DIM = HEAD_DIM // 2
GRID_W = 64
ROPE_BASE = 10000.0
N_EXPERTS = 32
N_EXPERT_GROUPS = 8
EXPERTS_PER_GROUP = 4
MOE_BLOCK = 256
NORM_EPS = 1e-6

ROW_TILE = 256
OUT_TILE = 512
OUT_SUB_TILE = 128
REC_CHUNK = 128
REC_SUB = 16
ATT_Q_TILE = {"gqa": 512, "diff": 256}
ATT_KV_CHUNK_MAX = 1536
VMEM_LIMIT = 56 * 1024 * 1024

NT_DIMS = (((1,), (1,)), ((), ()))
LOG2E = math.log2(math.e)


def _sigmoid(v):
    return 1.0 / (1.0 + jnp.exp(-v))


def _cparams(*sem):
    return pltpu.CompilerParams(dimension_semantics=sem, vmem_limit_bytes=VMEM_LIMIT)


def _mod_kernel(c_ref, w_ref, b_ref, o_ref):
    cv = c_ref[...]
    s = cv * _sigmoid(cv)
    o_ref[...] = jnp.dot(s, w_ref[...], precision=HIGHEST, preferred_element_type=F32) + b_ref[...]


def _mod_call(cvec, w_mod, b_mod):
    d, n = w_mod.shape
    tn = 1024
    return pl.pallas_call(
        _mod_kernel,
        grid=(n // tn,),
        in_specs=[pl.BlockSpec((8, d), lambda j: (0, 0)),
                  pl.BlockSpec((d, tn), lambda j: (0, j)),
                  pl.BlockSpec((1, tn), lambda j: (0, j))],
        out_specs=pl.BlockSpec((8, tn), lambda j: (0, j)),
        out_shape=jax.ShapeDtypeStruct((8, n), F32),
        compiler_params=_cparams("arbitrary"),
        name="mod",
    )(cvec, w_mod, b_mod.reshape(1, n))


def _rope(v, cos, sin_a, sin_b, half):
    n = v.shape[-1]
    return v * cos + pltpu.roll(v, n - half, 1) * sin_a + pltpu.roll(v, half, 1) * sin_b


def _in_proj_kernel(x_ref, mod_ref, g1_ref, w_ref, wvt_ref, c64_ref, sa64_ref, sb64_ref,
                    c32_ref, sa32_ref, sb32_ref, qg_ref, kg_ref, seg_ref,
                    zr_ref, zg_ref, zh_ref, zd_ref, vt_ref, gk_ref, gv_ref, kd_ref, vd_ref):
    d = x_ref.shape[-1]
    gw = GROUP_WIDTH
    x = x_ref[...]
    ms = jnp.mean(x * x, axis=-1, keepdims=True)
    h = x * lax.rsqrt(ms + NORM_EPS) * g1_ref[...]
    mod = mod_ref[0]
    h = h * (1.0 + mod[:, d:2 * d]) + mod[:, 0:d]
    hb = h.astype(BF16)

    def proj(c0, c1):
        return jnp.dot(hb, w_ref[:, c0:c1], preferred_element_type=F32)

    c64, sa64, sb64 = c64_ref[...], sa64_ref[...], sb64_ref[...]
    c32, sa32, sb32 = c32_ref[...], sa32_ref[...], sb32_ref[...]
    seg = seg_ref[...]

    zr_ref[:, 0:gw] = _rope(proj(0, gw) * HEAD_DIM ** -0.5, c64, sa64, sb64, 32)
    zr_ref[:, gw:2 * gw] = _rope(proj(gw, 2 * gw), c64, sa64, sb64, 32)
    zr_ref[:, 2 * gw:4 * gw] = proj(2 * gw, 4 * gw)

    o = 4 * gw
    q = proj(o, o + gw)
    qms = jnp.dot(q * q, seg, precision=HIGHEST, preferred_element_type=F32)
    q = q * lax.rsqrt(qms + NORM_EPS) * qg_ref[...]
    q = _rope(q, c64, sa64, sb64, 32) * (HEAD_DIM ** -0.5 * LOG2E)
    lane = lax.broadcasted_iota(jnp.int32, (q.shape[0], 128), 1)
    lo = lane < HEAD_DIM
    qa, qb = q[:, 0:128], q[:, 128:256]
    zg_ref[:, 0:128] = jnp.where(lo, qa, 0.0).astype(BF16)
    zg_ref[:, 128:256] = jnp.where(lo, pltpu.roll(qa, 64, 1), 0.0).astype(BF16)
    zg_ref[:, 256:384] = jnp.where(lo, 0.0, pltpu.roll(qb, 64, 1)).astype(BF16)
    zg_ref[:, 384:512] = jnp.where(lo, 0.0, qb).astype(BF16)
    k = proj(o + gw, o + gw + 128)
    kms = jnp.dot(k * k, seg[0:128, 0:128], precision=HIGHEST, preferred_element_type=F32)
    k = k * lax.rsqrt(kms + NORM_EPS) * kg_ref[...]
    gk_ref[...] = _rope(k, c64[:, 0:128], sa64[:, 0:128], sb64[:, 0:128], 32).astype(BF16)
    v = proj(o + gw + 128, o + gw + 256)
    gv_ref[0] = jnp.where(lo, v, 1.0).astype(BF16)
    gv_ref[1] = jnp.where(lo, 1.0, v).astype(BF16)

    o = 6 * gw
    zh_ref[...] = proj(o, o + 5 * gw)

    o = 11 * gw
    zd_ref[...] = (_rope(proj(o, o + gw), c32, sa32, sb32, 16) * (DIFF_QK_DIM ** -0.5 * LOG2E)).astype(BF16)
    kd_ref[...] = _rope(proj(o + gw, o + 2 * gw), c32, sa32, sb32, 16).astype(BF16)
    v = proj(o + 2 * gw, o + 3 * gw)
    va, vb = v[:, 0:128], v[:, 128:256]
    vd_ref[0] = jnp.where(lo, va, 1.0).astype(BF16)
    vd_ref[1] = jnp.where(lo, pltpu.roll(va, 64, 1), 1.0).astype(BF16)
    vd_ref[2] = jnp.where(lo, vb, 1.0).astype(BF16)
    vd_ref[3] = jnp.where(lo, pltpu.roll(vb, 64, 1), 1.0).astype(BF16)

    vt_ref[...] = lax.dot_general(wvt_ref[...], hb, NT_DIMS, preferred_element_type=F32).astype(BF16)


def _in_proj_call(x_all, mod_all, g1, w_bf, wvt_bf, tabs, qg, kg, seg, n_lat_rows, t_len):
    r, d = x_all.shape
    tm = ROW_TILE
    gw = GROUP_WIDTH
    n_lat_tiles = n_lat_rows // tm
    tiles_per_seq = t_len // tm
    n_batch = n_lat_rows // t_len

    def mod_idx(i):
        return (jnp.where(i < n_lat_tiles, i // tiles_per_seq, n_batch), 0, 0)

    def tab_idx(i):
        return (jnp.where(i < n_lat_tiles, i % tiles_per_seq, tiles_per_seq), 0)

    ctx_tiles = (r - n_lat_rows) // n_batch // tm
    seq_tiles = tiles_per_seq + ctx_tiles

    def kv_idx(i):
        ci = i - n_lat_tiles
        lat = (i // tiles_per_seq) * seq_tiles + ctx_tiles + i % tiles_per_seq
        ctx = (ci // ctx_tiles) * seq_tiles + ci % ctx_tiles
        return (jnp.where(i < n_lat_tiles, lat, ctx), 0)

    row = lambda i: (i, 0)
    const = lambda i: (0, 0)
    tab_spec = pl.BlockSpec((tm, gw), tab_idx)
    return pl.pallas_call(
        _in_proj_kernel,
        grid=(r // tm,),
        in_specs=[pl.BlockSpec((tm, d), row),
                  pl.BlockSpec((1, 1, mod_all.shape[-1]), mod_idx),
                  pl.BlockSpec((1, d), const),
                  pl.BlockSpec(w_bf.shape, const),
                  pl.BlockSpec(wvt_bf.shape, const),
                  tab_spec, tab_spec, tab_spec, tab_spec, tab_spec, tab_spec,
                  pl.BlockSpec((1, gw), const),
                  pl.BlockSpec((1, 128), const),
                  pl.BlockSpec((gw, gw), const)],
        out_specs=[pl.BlockSpec((tm, 4 * gw), row),
                   pl.BlockSpec((tm, 2 * gw), row),
                   pl.BlockSpec((tm, 5 * gw), row),
                   pl.BlockSpec((tm, gw), row),
                   pl.BlockSpec((2 * gw, tm), lambda i: (0, i)),
                   pl.BlockSpec((tm, 128), kv_idx),
                   pl.BlockSpec((GQA_KV_HEADS, tm, 128), lambda i: (0,) + kv_idx(i)),
                   pl.BlockSpec((tm, gw), kv_idx),
                   pl.BlockSpec((GROUP_HEADS, tm, 128), lambda i: (0,) + kv_idx(i))],
        out_shape=[jax.ShapeDtypeStruct((r, 4 * gw), F32),
                   jax.ShapeDtypeStruct((r, 2 * gw), BF16),
                   jax.ShapeDtypeStruct((r, 5 * gw), F32),
                   jax.ShapeDtypeStruct((r, gw), BF16),
                   jax.ShapeDtypeStruct((2 * gw, r), BF16),
                   jax.ShapeDtypeStruct((r, 128), BF16),
                   jax.ShapeDtypeStruct((GQA_KV_HEADS, r, 128), BF16),
                   jax.ShapeDtypeStruct((r, gw), BF16),
                   jax.ShapeDtypeStruct((GROUP_HEADS, r, 128), BF16)],
        compiler_params=_cparams("arbitrary"),
        name="in_proj",
    )(x_all, mod_all, g1, w_bf, wvt_bf, *tabs, qg, kg, seg)


def _rec_direction(q_ref, k_ref, v_ref, vt_ref, par, bd_ref, o_ref, s_ref,
                   lc_ref, qt_ref, kh_ref, kk_ref, tot_ref, a_ref, *, reverse):
    c, gw = q_ref.shape
    sub = REC_SUB
    n_sub = c // sub
    sig = _sigmoid(k_ref[...])
    logf = jnp.log(par + (1.0 - par) * sig)
    kk = (1.0 - par) * (1.0 - sig)
    r_i = lax.broadcasted_iota(jnp.int32, (c, c), 0)
    c_i = lax.broadcasted_iota(jnp.int32, (c, c), 1)
    shift = sub.bit_length() - 1
    same = jnp.right_shift(r_i, shift) == jnp.right_shift(c_i, shift)
    if reverse:
        incl = same & (c_i >= r_i)
        excl = same & (c_i < r_i)
    else:
        incl = same & (c_i <= r_i)
        excl = same & (c_i > r_i)
    lc = jnp.dot(jnp.where(incl, 1.0, 0.0), logf, precision=HIGHEST, preferred_element_type=F32)
    rr = jnp.dot(jnp.where(excl, 1.0, 0.0), logf, precision=HIGHEST, preferred_element_type=F32)
    lc_ref[...] = lc * LOG2E
    qt_ref[...] = (q_ref[...] * jnp.exp(lc)).astype(BF16)
    kh_ref[...] = kk * jnp.exp(rr)
    kk_ref[...] = kk
    tot_ref[...] = lc + rr
    bd = bd_ref[...]
    t_loc = lax.broadcasted_iota(jnp.int32, (sub, gw), 0)
    row_id = lax.broadcasted_iota(jnp.int32, (c, gw), 0)
    half = gw // 2
    head_shift = HEAD_DIM.bit_length() - 1
    same_head = (jnp.right_shift(lax.broadcasted_iota(jnp.int32, (half, half), 0), head_shift)
                 == jnp.right_shift(lax.broadcasted_iota(jnp.int32, (half, half), 1), head_shift))
    quads = (slice(0, half), slice(half, gw))

    def step(j):
        a = (n_sub - 1 - j) if reverse else j
        base = a * sub
        rows = pl.ds(base, sub)
        s_q = [s_ref[qd, qd] for qd in quads]
        qt_a = qt_ref[rows, :]
        o_inter = jnp.concatenate(
            [lax.dot_general(qt_a[:, qd], s.astype(BF16), NT_DIMS, preferred_element_type=F32)
             for qd, s in zip(quads, s_q)], axis=1)
        lc_a = lc_ref[rows, :]
        q_a = q_ref[rows, :]
        k_a = kk_ref[rows, :]
        v_a = v_ref[rows, :]
        for s in range(sub):
            valid = (t_loc <= s) if reverse else (t_loc >= s)
            e = jnp.where(valid, jnp.exp2(lc_a - lc_a[s:s + 1, :]), 0.0)
            a_ref[a, s * sub:(s + 1) * sub, :] = (q_a * e * k_a[s:s + 1, :]).astype(BF16)
        b = jnp.dot(a_ref[a], bd, preferred_element_type=F32)
        o_intra = jnp.zeros((sub, gw), F32)
        for s in range(sub):
            o_intra = o_intra + b[s * sub:(s + 1) * sub, :] * v_a[s:s + 1, :]
        o_ref[rows, :] = o_inter + o_intra
        in_sub = (row_id >= base) & (row_id < base + sub)
        kh_m = jnp.where(in_sub, kh_ref[...], 0.0).astype(BF16)
        decay = jnp.exp(tot_ref[pl.ds(base, 1), :])
        for qd, s in zip(quads, s_q):
            u = jnp.dot(vt_ref[qd, :], kh_m[:, qd], preferred_element_type=F32)
            s_ref[qd, qd] = s * decay[:, qd] + jnp.where(same_head, u, 0.0)

    return step, n_sub


def _rec_kernel(qf_ref, kf_ref, vf_ref, vtf_ref, qb_ref, kb_ref, vb_ref, vtb_ref, par_ref, bd_ref,
                of_ref, ob_ref, sf_ref, sb_ref, *tmp):
    @pl.when(pl.program_id(1) == 0)
    def _():
        sf_ref[...] = jnp.zeros_like(sf_ref)
        sb_ref[...] = jnp.zeros_like(sb_ref)

    n_tmp = len(tmp) // 2
    step_f, n_sub = _rec_direction(qf_ref, kf_ref, vf_ref, vtf_ref, par_ref[0:1, :], bd_ref, of_ref,
                                   sf_ref, *tmp[:n_tmp], reverse=False)
    step_b, _ = _rec_direction(qb_ref, kb_ref, vb_ref, vtb_ref, par_ref[1:2, :], bd_ref, ob_ref,
                               sb_ref, *tmp[n_tmp:], reverse=True)
    for j in range(n_sub):
        step_f(j)
        step_b(j)


def _ret_direction(q_ref, k_ref, v_ref, vt_ref, logg, o_ref, s_ref, *, reverse):
    c, gw = q_ref.shape
    q = q_ref[...]
    k = k_ref[...]
    t_row = lax.broadcasted_iota(jnp.int32, (c, gw), 0).astype(F32)
    if reverse:
        n_q, n_k = c - t_row, t_row
    else:
        n_q, n_k = t_row + 1.0, (c - 1.0) - t_row
    s_t = s_ref[...]
    qt = (q * jnp.exp(n_q * logg)).astype(BF16)
    o = lax.dot_general(qt, s_t.astype(BF16), NT_DIMS, preferred_element_type=F32)
    r_i = lax.broadcasted_iota(jnp.int32, (c, c), 0)
    c_i = lax.broadcasted_iota(jnp.int32, (c, c), 1)
    dist = (c_i - r_i) if reverse else (r_i - c_i)
    allowed = dist >= 0
    dist_f = dist.astype(F32)
    lane = lax.broadcasted_iota(jnp.int32, (c, gw), 1)
    kb = k.astype(BF16)
    vb = v_ref[...].astype(BF16)
    for h in range(GROUP_HEADS):
        in_h = (lane >= h * HEAD_DIM) & (lane < (h + 1) * HEAD_DIM)
        mask = jnp.where(allowed, jnp.exp(dist_f * logg[:, h * HEAD_DIM:h * HEAD_DIM + 1]), 0.0)
        qh = jnp.where(in_h, q, 0.0).astype(BF16)
        sc = lax.dot_general(qh, kb, NT_DIMS, preferred_element_type=F32) * mask
        oh = jnp.dot(sc.astype(BF16), vb, preferred_element_type=F32)
        o = o + jnp.where(in_h, oh, 0.0)
    o_ref[...] = o
    kh = (k * jnp.exp(n_k * logg)).astype(BF16)
    u = jnp.dot(vt_ref[...], kh, preferred_element_type=F32)
    head_shift = HEAD_DIM.bit_length() - 1
    same_head = (jnp.right_shift(lax.broadcasted_iota(jnp.int32, (gw, gw), 0), head_shift)
                 == jnp.right_shift(lax.broadcasted_iota(jnp.int32, (gw, gw), 1), head_shift))
    s_ref[...] = s_t * jnp.exp(float(c) * logg) + jnp.where(same_head, u, 0.0)


def _ret_kernel(qf_ref, kf_ref, vf_ref, vtf_ref, qb_ref, kb_ref, vb_ref, vtb_ref, par_ref, bd_ref,
                of_ref, ob_ref, sf_ref, sb_ref):
    @pl.when(pl.program_id(1) == 0)
    def _():
        sf_ref[...] = jnp.zeros_like(sf_ref)
        sb_ref[...] = jnp.zeros_like(sb_ref)

    _ret_direction(qf_ref, kf_ref, vf_ref, vtf_ref, par_ref[0:1, :], of_ref, sf_ref, reverse=False)
    _ret_direction(qb_ref, kb_ref, vb_ref, vtb_ref, par_ref[1:2, :], ob_ref, sb_ref, reverse=True)


def _rec_call(z, vt_all, par, bd, *, hgrn, n_batch, t_len, ctx_len, cols, vt_row):
    r = z.shape[0]
    c = REC_CHUNK
    gw = GROUP_WIDTH
    n_ctx = ctx_len // c
    n_lat = t_len // c
    lat_blocks = n_batch * n_lat

    def fwd_blk(b, i):
        return jnp.where(i < n_ctx, lat_blocks + b * n_ctx + i, b * n_lat + i - n_ctx)

    def bwd_blk(b, i):
        return jnp.where(i < n_ctx, lat_blocks + b * n_ctx + (n_ctx - 1 - i),
                         b * n_lat + (n_lat - 1 - (i - n_ctx)))

    def zspec(blk, col):
        return pl.BlockSpec((c, gw), lambda b, i: (blk(b, i), col))

    def vtspec(blk):
        return pl.BlockSpec((gw, c), lambda b, i: (vt_row, blk(b, i)))

    const = lambda b, i: (0, 0)
    cq, ckf, ckb, cv = cols
    state_scratch = [pltpu.VMEM((gw, gw), F32), pltpu.VMEM((gw, gw), F32)]
    if hgrn:
        body = _rec_kernel
        scratch = state_scratch + 2 * [
            pltpu.VMEM((c, gw), F32), pltpu.VMEM((c, gw), BF16), pltpu.VMEM((c, gw), F32),
            pltpu.VMEM((c, gw), F32), pltpu.VMEM((c, gw), F32),
            pltpu.VMEM((c // REC_SUB, REC_SUB * REC_SUB, gw), BF16)]
    else:
        body = _ret_kernel
        scratch = state_scratch
    out_f, out_b = pl.pallas_call(
        body,
        grid=(n_batch, n_ctx + n_lat),
        in_specs=[zspec(fwd_blk, cq), zspec(fwd_blk, ckf), zspec(fwd_blk, cv), vtspec(fwd_blk),
                  zspec(bwd_blk, cq), zspec(bwd_blk, ckb), zspec(bwd_blk, cv), vtspec(bwd_blk),
                  pl.BlockSpec((2, gw), const),
                  pl.BlockSpec((gw, gw), const)],
        out_specs=[pl.BlockSpec((c, gw), lambda b, i: (fwd_blk(b, i), 0)),
                   pl.BlockSpec((c, gw), lambda b, i: (bwd_blk(b, i), 0))],
        out_shape=[jax.ShapeDtypeStruct((r, gw), F32), jax.ShapeDtypeStruct((r, gw), F32)],
        scratch_shapes=scratch,
        compiler_params=_cparams("arbitrary", "arbitrary"),
        name="hgrn_rec" if hgrn else "ret_rec",
    )(z, z, z, vt_all, z, z, z, vt_all, par, bd)
    return out_f, out_b


def _online_softmax(lhs_ref, k_ref, v_ref, kv_chunk, scr):
    m_ref, acc_ref = scr[0:2]
    s_refs, p_refs, a_refs, x_refs = scr[2:4], scr[4:6], scr[6:8], scr[8:10]
    n_heads = lhs_ref.shape[0]
    n = k_ref.shape[0] // kv_chunk
    steps = n_heads * n
    assert steps % 2 == 0 and steps >= 4
    m_ref[...] = jnp.full(m_ref.shape, -jnp.inf, F32)
    acc_ref[...] = jnp.zeros(acc_ref.shape, F32)

    def split(t):
        if isinstance(t, int):
            return t // n, pl.ds((t % n) * kv_chunk, kv_chunk)
        h = t // n
        return h, pl.ds(pl.multiple_of((t - h * n) * kv_chunk, kv_chunk), kv_chunk)

    def qk(t, slot):
        h, rows = split(t)
        s = lax.dot_general(lhs_ref[h], k_ref[rows, :], NT_DIMS, preferred_element_type=F32)
        s_refs[slot][...] = s
        x_refs[slot][...] = jnp.max(s, axis=-1, keepdims=True)

    def sm(t, slot):
        h, _ = split(t)
        s = s_refs[slot][...]
        m = m_ref[h]
        m_new = jnp.maximum(m, x_refs[slot][...])
        m_ref[h] = m_new
        a_refs[slot][...] = jnp.exp2(m - m_new)
        p_refs[slot][...] = jnp.exp2((s - m_new).astype(BF16))

    def pv(t, slot):
        h, rows = split(t)
        v_blk = v_ref[rows, :] if len(v_ref.shape) == 2 else v_ref[h, rows, :]
        acc_ref[h] = a_refs[slot][...] * acc_ref[h] + jnp.dot(
            p_refs[slot][...], v_blk, preferred_element_type=F32)

    qk(0, 0)
    qk(1, 1)
    sm(0, 0)

    def body(i, carry):
        t = 2 * i + 1
        qk(t + 1, 0)
        sm(t, 1)
        pv(t - 1, 0)
        qk(t + 2, 1)
        sm(t + 1, 0)
        pv(t, 1)
        return carry

    lax.fori_loop(0, (steps - 2) // 2, body, 0)
    sm(steps - 1, 1)
    pv(steps - 2, 0)
    pv(steps - 1, 1)


def _gqa_kernel(q_ref, k_ref, v_ref, o_ref, lhs_ref, *scr, kv_chunk):
    tq = q_ref.shape[0]
    lane = lax.broadcasted_iota(jnp.int32, (tq, 128), 1)
    lo = lane < HEAD_DIM
    for j in range(GQA_KV_HEADS):
        lhs_ref[j, 0:tq, :] = q_ref[:, (2 * j) * 128:(2 * j + 1) * 128]
        lhs_ref[j, tq:2 * tq, :] = q_ref[:, (2 * j + 1) * 128:(2 * j + 2) * 128]
    _online_softmax(lhs_ref, k_ref, v_ref, kv_chunk, scr)
    acc_ref = scr[1]
    lo2 = lax.broadcasted_iota(jnp.int32, (2 * tq, 128), 1) < HEAD_DIM
    acc0, acc1 = acc_ref[0], acc_ref[1]
    o0 = acc0 / jnp.where(lo2, pltpu.roll(acc0, 64, 1), 1.0)
    o1 = acc1 / jnp.where(lo2, 1.0, pltpu.roll(acc1, 64, 1))
    o_ref[:, 0:128] = jnp.where(lo, o0[0:tq], pltpu.roll(o0[tq:2 * tq], 64, 1))
    o_ref[:, 128:256] = jnp.where(lo, pltpu.roll(o1[0:tq], 64, 1), o1[tq:2 * tq])


def _diff_kernel(lam_ref, q_ref, k_ref, v_ref, ng_ref, seg_ref, o_ref, lhs_ref, *scr, kv_chunk,
                 out_scale):
    tq, gw = q_ref.shape
    lane = lax.broadcasted_iota(jnp.int32, (tq, gw), 1)
    q = q_ref[...].astype(F32)
    lam = lam_ref[0, 0]
    for h in range(GROUP_HEADS):
        base = h * HEAD_DIM
        q1 = jnp.where((lane >= base) & (lane < base + DIFF_QK_DIM), q, 0.0)
        q2 = jnp.where((lane >= base + DIFF_QK_DIM) & (lane < base + HEAD_DIM), q, 0.0)
        lhs_ref[h, 0:tq, :] = q1.astype(BF16)
        lhs_ref[h, tq:2 * tq, :] = q2.astype(BF16)
    _online_softmax(lhs_ref, k_ref, v_ref, kv_chunk, scr)
    acc_ref = scr[1]
    lo = lax.broadcasted_iota(jnp.int32, (tq, 128), 1) < HEAD_DIM
    lo2 = lax.broadcasted_iota(jnp.int32, (2 * tq, 128), 1) < HEAD_DIM
    heads = []
    for h in range(GROUP_HEADS):
        acc = acc_ref[h]
        o = acc / jnp.where(lo2, pltpu.roll(acc, 64, 1), 1.0)
        heads.append(o[0:tq] - lam * o[tq:2 * tq])
    out = jnp.concatenate([jnp.where(lo, heads[0], pltpu.roll(heads[1], 64, 1)),
                           jnp.where(lo, heads[2], pltpu.roll(heads[3], 64, 1))], axis=1)
    ms = jnp.dot(out * out, seg_ref[...], precision=HIGHEST, preferred_element_type=F32)
    o_ref[...] = out * lax.rsqrt(ms + NORM_EPS) * ng_ref[...] * out_scale


def _kv_chunk(s_len, n_heads):
    best = None
    for kc in range(128, min(s_len, ATT_KV_CHUNK_MAX) + 1, 128):
        if s_len % kc == 0 and n_heads * (s_len // kc) >= 4:
            best = kc
    assert best is not None, s_len
    return best


def _attn_call(kind, zq, kv, *, n_batch, t_len, ctx_len, ctx_queries, lam=None, norm_g=None,
               seg=None, out_scale=None):
    gw = GROUP_WIDTH
    lat_rows = n_batch * t_len
    s_all = ctx_len + t_len
    if kind == "gqa":
        qw, kw, n_heads = 512, 128, GQA_KV_HEADS
    else:
        qw, kw, n_heads = 256, 256, GROUP_HEADS
    if ctx_queries:
        tq, s_len = ctx_len, ctx_len
        grid = (n_batch, 1)
        q_map = lambda b, i: (lat_rows // ctx_len + b, 0)
        out_map = lambda b, i: (b, 0)
        kv_blk = lambda b: b * (s_all // ctx_len)
    else:
        tq, s_len = ATT_Q_TILE[kind], s_all
        grid = (n_batch, t_len // tq)
        q_map = lambda b, i: (b * (t_len // tq) + i, 0)
        out_map = q_map
        kv_blk = lambda b: b
    kc = _kv_chunk(s_len, n_heads)
    nv = 128
    kv_specs = [pl.BlockSpec((s_len, kw), lambda b, i: (kv_blk(b), 0)),
                pl.BlockSpec((n_heads, s_len, nv), lambda b, i: (0, kv_blk(b), 0))]
    kv_args = kv
    out_spec = pl.BlockSpec((tq, gw), out_map)
    out_shape = jax.ShapeDtypeStruct((n_batch * tq * grid[1], gw), F32)
    q_spec = pl.BlockSpec((tq, qw), q_map)
    const = lambda b, i: (0, 0)
    m_rows = 2 * tq
    scratch = [pltpu.VMEM((n_heads, m_rows, kw), BF16),
               pltpu.VMEM((n_heads, m_rows, 1), F32),
               pltpu.VMEM((n_heads, m_rows, nv), F32),
               pltpu.VMEM((m_rows, kc), F32), pltpu.VMEM((m_rows, kc), F32),
               pltpu.VMEM((m_rows, kc), BF16), pltpu.VMEM((m_rows, kc), BF16),
               pltpu.VMEM((m_rows, 1), F32), pltpu.VMEM((m_rows, 1), F32),
               pltpu.VMEM((m_rows, 1), F32), pltpu.VMEM((m_rows, 1), F32)]
    if kind == "gqa":
        return pl.pallas_call(
            functools.partial(_gqa_kernel, kv_chunk=kc),
            grid=grid, in_specs=[q_spec] + kv_specs, out_specs=out_spec, out_shape=out_shape,
            scratch_shapes=scratch,
            compiler_params=_cparams("arbitrary", "arbitrary"),
            name="gqa_ctx" if ctx_queries else "gqa_lat",
        )(zq, *kv_args)
    return pl.pallas_call(
        functools.partial(_diff_kernel, kv_chunk=kc, out_scale=out_scale),
        grid=grid,
        in_specs=[pl.BlockSpec(memory_space=pltpu.SMEM), q_spec] + kv_specs
        + [pl.BlockSpec((1, gw), const), pl.BlockSpec((gw, gw), const)],
        out_specs=out_spec, out_shape=out_shape,
        scratch_shapes=scratch,
        compiler_params=_cparams("arbitrary", "arbitrary"),
        name="diff_ctx" if ctx_queries else "diff_lat",
    )(lam, zq, *kv_args, norm_g, seg)


def _out_proj_kernel(rf_ref, rb_ref, rg_ref, ga_ref, hf_ref, hb_ref, hg_ref, df_ref, x_ref, mod_ref,
                     w_ref, rn_ref, hn_ref, seg_ref, n2_ref, rw_ref, rbias_ref,
                     xo_ref, h2_ref, e_ref, gt_ref):
    sub = OUT_SUB_TILE
    for i0 in range(0, x_ref.shape[0], sub):
        _out_proj_rows(slice(i0, i0 + sub), rf_ref, rb_ref, rg_ref, ga_ref, hf_ref, hb_ref, hg_ref,
                       df_ref, x_ref, mod_ref, w_ref, rn_ref, hn_ref, seg_ref, n2_ref, rw_ref,
                       rbias_ref, xo_ref, h2_ref, e_ref, gt_ref)


def _out_proj_rows(rs, rf_ref, rb_ref, rg_ref, ga_ref, hf_ref, hb_ref, hg_ref, df_ref, x_ref, mod_ref,
                   w_ref, rn_ref, hn_ref, seg_ref, n2_ref, rw_ref, rbias_ref,
                   xo_ref, h2_ref, e_ref, gt_ref):
    d = x_ref.shape[-1]
    gw = GROUP_WIDTH
    seg = seg_ref[...]
    mod = mod_ref[0]

    def readout(o, norm_g, gate):
        ms = jnp.dot(o * o, seg, precision=HIGHEST, preferred_element_type=F32)
        return o * lax.rsqrt(ms + NORM_EPS) * norm_g * (gate * _sigmoid(gate))

    parts = (readout(rf_ref[rs, :] + rb_ref[rs, :], rn_ref[...], rg_ref[rs, :]),
             ga_ref[rs, :],
             readout(hf_ref[rs, :] + hb_ref[rs, :], hn_ref[...], hg_ref[rs, :]),
             df_ref[rs, :])
    acc = jnp.zeros((rs.stop - rs.start, d), F32)
    for n, part in enumerate(parts):
        acc = acc + jnp.dot(part.astype(BF16), w_ref[n * gw:(n + 1) * gw, :],
                            preferred_element_type=F32)
    x = x_ref[rs, :] + mod[:, 2 * d:3 * d] * acc
    xo_ref[rs, :] = x
    ms = jnp.mean(x * x, axis=-1, keepdims=True)
    h2 = x * lax.rsqrt(ms + NORM_EPS) * n2_ref[...]
    h2 = h2 * (1.0 + mod[:, 4 * d:5 * d]) + mod[:, 3 * d:4 * d]
    h2_ref[rs, :] = h2

    logits = lax.dot_general(rw_ref[...], h2, NT_DIMS, precision=HIGHEST, preferred_element_type=F32)
    score = _sigmoid(logits)
    sel = score + rbias_ref[:, 0:1]
    ng = N_EXPERT_GROUPS
    sc = [score[m * ng:(m + 1) * ng, :] for m in range(EXPERTS_PER_GROUP)]
    sl = [sel[m * ng:(m + 1) * ng, :] for m in range(EXPERTS_PER_GROUP)]
    hi1, lo1 = jnp.maximum(sl[0], sl[1]), jnp.minimum(sl[0], sl[1])
    hi2, lo2 = jnp.maximum(sl[2], sl[3]), jnp.minimum(sl[2], sl[3])
    group_score = jnp.maximum(hi1, hi2) + jnp.maximum(jnp.minimum(hi1, hi2), jnp.maximum(lo1, lo2))
    g_iota = lax.broadcasted_iota(jnp.int32, group_score.shape, 0).astype(F32)
    g_max = jnp.max(group_score, axis=0, keepdims=True)
    best = jnp.min(jnp.where(group_score == g_max, g_iota, float(ng)), axis=0, keepdims=True)
    pick = g_iota == best
    v = [jnp.sum(jnp.where(pick, a, 0.0), axis=0, keepdims=True) for a in sl]
    u = [jnp.sum(jnp.where(pick, a, 0.0), axis=0, keepdims=True) for a in sc]
    rank = []
    for i in range(EXPERTS_PER_GROUP):
        r_i = jnp.zeros(best.shape, F32)
        for j in range(EXPERTS_PER_GROUP):
            if j < i:
                r_i = r_i + jnp.where(v[j] >= v[i], 1.0, 0.0)
            elif j > i:
                r_i = r_i + jnp.where(v[j] > v[i], 1.0, 0.0)
        rank.append(r_i)
    idx, gate = [], []
    for kth in range(2):
        idx.append(sum(jnp.where(rank[i] == kth, float(i), 0.0) for i in range(EXPERTS_PER_GROUP)))
        gate.append(sum(jnp.where(rank[i] == kth, u[i], 0.0) for i in range(EXPERTS_PER_GROUP)))
    tot = gate[0] + gate[1]
    e_ref[0:1, rs] = (best * EXPERTS_PER_GROUP + idx[0]).astype(jnp.int32)
    e_ref[1:2, rs] = (best * EXPERTS_PER_GROUP + idx[1]).astype(jnp.int32)
    gt_ref[0:1, rs] = gate[0] / tot
    gt_ref[1:2, rs] = gate[1] / tot


def _out_proj_call(rf, rb, zr, o_gqa, hf, hb, zh, o_df, x_all, mod_all, w_out_bf, rn, hn, seg, n2,
                   rw_t, rbias, n_rows, n_lat_rows, t_len):
    d = x_all.shape[-1]
    tm = OUT_TILE
    gw = GROUP_WIDTH
    n_lat_tiles = n_lat_rows // tm
    tiles_per_seq = t_len // tm
    n_batch = n_lat_rows // t_len

    def mod_idx(i):
        return (jnp.where(i < n_lat_tiles, i // tiles_per_seq, n_batch), 0, 0)

    row = lambda i: (i, 0)
    const = lambda i: (0, 0)
    blk = lambda col: pl.BlockSpec((tm, gw), lambda i: (i, col))
    return pl.pallas_call(
        _out_proj_kernel,
        grid=(n_rows // tm,),
        in_specs=[blk(0), blk(0), blk(3), blk(0), blk(0), blk(0), blk(4), blk(0),
                  pl.BlockSpec((tm, d), row),
                  pl.BlockSpec((1, 1, mod_all.shape[-1]), mod_idx),
                  pl.BlockSpec((4 * gw, d), const),
                  pl.BlockSpec((1, gw), const), pl.BlockSpec((1, gw), const),
                  pl.BlockSpec((gw, gw), const),
                  pl.BlockSpec((1, d), const),
                  pl.BlockSpec((N_EXPERTS, d), const),
                  pl.BlockSpec((N_EXPERTS, 128), const)],
        out_specs=[pl.BlockSpec((tm, d), row), pl.BlockSpec((tm, d), row),
                   pl.BlockSpec((2, tm), lambda i: (0, i)), pl.BlockSpec((2, tm), lambda i: (0, i))],
        out_shape=[jax.ShapeDtypeStruct((n_rows, d), F32), jax.ShapeDtypeStruct((n_rows, d), F32),
                   jax.ShapeDtypeStruct((2, n_rows), jnp.int32),
                   jax.ShapeDtypeStruct((2, n_rows), F32)],
        compiler_params=_cparams("arbitrary"),
        name="out_proj",
    )(rf, rb, zr, o_gqa, hf, hb, zh, o_df, x_all, mod_all, w_out_bf, rn, hn, seg, n2, rw_t, rbias)


def _expert_kernel(be_ref, nb_ref, idx_hbm, gate_ref, h2_hbm, wg_ref, wu_ref, wd_ref, ys_hbm,
                   idx_ref, x_ref, y_ref, wgb_ref, wub_ref, wdb_ref, sem_i, sem_g, sem_s):
    i = pl.program_id(0)
    nb = nb_ref[0]
    rows = MOE_BLOCK

    def idx_copy(blk):
        return pltpu.make_async_copy(idx_hbm.at[blk], idx_ref.at[blk % 3], sem_i.at[blk % 3])

    def issue_gather(blk, xslot):
        islot = blk % 3
        for r in range(rows):
            pltpu.make_async_copy(h2_hbm.at[pl.ds(idx_ref[islot, r], 1)],
                                  x_ref.at[xslot, pl.ds(r, 1)], sem_g.at[xslot]).start()

    def wait_slot(buf_ref, sem, slot):
        pltpu.make_async_copy(buf_ref.at[slot], buf_ref.at[slot], sem.at[slot]).wait()

    def step(slot):
        @pl.when(i == 0)
        def _():
            y_ref[...] = jnp.zeros(y_ref.shape, F32)
            n_real = ys_hbm.shape[0] - 2 * rows
            for s in range(2):
                fill = pltpu.make_async_copy(y_ref.at[s], ys_hbm.at[pl.ds(n_real + s * rows, rows)],
                                             sem_s.at[s])
                fill.start()
                fill.wait()
            idx_copy(0).start()
            idx_copy(0).wait()
            issue_gather(0, 0)

            @pl.when(nb > 1)
            def _():
                idx_copy(1).start()

        @pl.when(i + 1 < nb)
        def _():
            idx_copy(i + 1).wait()
            issue_gather(i + 1, 1 - slot)

            @pl.when(i + 2 < nb)
            def _():
                idx_copy(i + 2).start()

        wait_slot(x_ref, sem_g, slot)

        @pl.when(i >= 2)
        def _():
            wait_slot(y_ref, sem_s, slot)

        @pl.when((i == 0) | (be_ref[i] != be_ref[jnp.maximum(i - 1, 0)]))
        def _():
            wgb_ref[...] = wg_ref[0].astype(BF16)
            wub_ref[...] = wu_ref[0].astype(BF16)
            wdb_ref[...] = wd_ref[0].astype(BF16)

        xb = x_ref[slot].astype(BF16)
        a = jnp.dot(xb, wgb_ref[...], preferred_element_type=F32)
        u = jnp.dot(xb, wub_ref[...], preferred_element_type=F32)
        hmid = (a * _sigmoid(a) * u).astype(BF16)
        y_ref[slot] = jnp.dot(hmid, wdb_ref[...], preferred_element_type=F32) * gate_ref[:, 0:1]

        islot = i % 3
        for r in range(rows):
            pltpu.make_async_copy(y_ref.at[slot, pl.ds(r, 1)],
                                  ys_hbm.at[pl.ds(idx_ref[islot, rows + r], 1)],
                                  sem_s.at[slot]).start(priority=r % 2)

        @pl.when(i == nb - 1)
        def _():
            @pl.when(i >= 1)
            def _():
                wait_slot(y_ref, sem_s, 1 - slot)
            wait_slot(y_ref, sem_s, slot)

    @pl.when(i < nb)
    def _():
        for parity in range(2):
            @pl.when(i % 2 == parity)
            def _(parity=parity):
                step(parity)


def _expert_call(block_expert, n_used, row_idx, row_gate, h2, wg, wu, wd, n_tok, layer):
    n_blocks = block_expert.shape[0]
    d = h2.shape[-1]
    ff = wg.shape[-1]
    rows = MOE_BLOCK
    grid_spec = pltpu.PrefetchScalarGridSpec(
        num_scalar_prefetch=2,
        grid=(n_blocks,),
        in_specs=[pl.BlockSpec(memory_space=pl.ANY),
                  pl.BlockSpec((rows, 1), lambda i, be, nb: (i, 0)),
                  pl.BlockSpec(memory_space=pl.ANY),
                  pl.BlockSpec((None, 1, d, ff), lambda i, be, nb: (layer, be[i], 0, 0)),
                  pl.BlockSpec((None, 1, d, ff), lambda i, be, nb: (layer, be[i], 0, 0)),
                  pl.BlockSpec((None, 1, ff, d), lambda i, be, nb: (layer, be[i], 0, 0))],
        out_specs=pl.BlockSpec(memory_space=pl.ANY),
        scratch_shapes=[pltpu.SMEM((3, 2 * rows), jnp.int32),
                        pltpu.VMEM((2, rows, d), F32),
                        pltpu.VMEM((2, rows, d), F32),
                        pltpu.VMEM((d, ff), BF16), pltpu.VMEM((d, ff), BF16), pltpu.VMEM((ff, d), BF16),
                        pltpu.SemaphoreType.DMA((3,)), pltpu.SemaphoreType.DMA((2,)),
                        pltpu.SemaphoreType.DMA((2,))],
    )
    return pl.pallas_call(
        _expert_kernel,
        grid_spec=grid_spec,
        out_shape=jax.ShapeDtypeStruct((2 * n_tok + 2 * rows, d), F32),
        compiler_params=_cparams("arbitrary"),
        name="experts",
    )(block_expert, n_used, row_idx, row_gate, h2, wg, wu, wd)


def _dispatch_plan(e_t, g_t, n_tok):
    m = 2 * n_tok
    e_flat = e_t.reshape(m)
    order = jnp.argsort(e_flat).astype(jnp.int32)
    experts = jnp.arange(N_EXPERTS, dtype=jnp.int32)
    counts = jnp.sum((e_flat[:, None] == experts[None, :]).astype(jnp.int32), axis=0)
    padded = (counts + MOE_BLOCK - 1) // MOE_BLOCK * MOE_BLOCK
    start = jnp.cumsum(counts) - counts
    pend = jnp.cumsum(padded)
    pstart = pend - padded
    n_blocks = -(-m // MOE_BLOCK) + N_EXPERTS
    blk_row0 = jnp.arange(n_blocks, dtype=jnp.int32) * MOE_BLOCK
    block_expert = jnp.minimum(
        jnp.sum((pend[None, :] <= blk_row0[:, None]).astype(jnp.int32), axis=1), N_EXPERTS - 1)
    blk = jnp.arange(n_blocks, dtype=jnp.int32)[:, None]
    r_in = jnp.arange(MOE_BLOCK, dtype=jnp.int32)[None, :]
    pos = blk_row0[:, None] + r_in - pstart[block_expert][:, None]
    valid = pos < counts[block_expert][:, None]
    sorted_pos = jnp.clip(start[block_expert][:, None] + pos, 0, m - 1)
    assign = order[sorted_pos]
    dump = m + (blk % 2) * MOE_BLOCK + r_in
    row_src = jnp.where(valid, assign % n_tok, 0)
    row_dst = jnp.where(valid, assign, dump)
    row_idx = jnp.concatenate([row_src, row_dst], axis=1).astype(jnp.int32)
    row_gate = jnp.where(valid, g_t.reshape(m)[assign], 0.0)
    n_used = (pend[-1] // MOE_BLOCK).astype(jnp.int32).reshape(1)
    return block_expert.astype(jnp.int32), n_used, row_idx, row_gate.reshape(n_blocks * MOE_BLOCK, 1)


def _combine_kernel(x_ref, y0_ref, y1_ref, mod_ref, fg_ref, o_ref, *, final_norm):
    d = x_ref.shape[-1]
    mod = mod_ref[0]
    x = x_ref[...] + mod[:, 5 * d:6 * d] * (y0_ref[...] + y1_ref[...])
    if final_norm:
        ms = jnp.mean(x * x, axis=-1, keepdims=True)
        x = x * lax.rsqrt(ms + NORM_EPS) * fg_ref[...]
    o_ref[...] = x


def _combine_call(x_new, ys, mod_all, final_g, n_rows, n_lat_rows, t_len, final_norm):
    d = x_new.shape[-1]
    tm = ROW_TILE
    n_lat_tiles = n_lat_rows // tm
    tiles_per_seq = t_len // tm
    n_batch = n_lat_rows // t_len
    n_tiles = n_rows // tm

    def mod_idx(i):
        return (jnp.where(i < n_lat_tiles, i // tiles_per_seq, n_batch), 0, 0)

    return pl.pallas_call(
        functools.partial(_combine_kernel, final_norm=final_norm),
        grid=(n_tiles,),
        in_specs=[pl.BlockSpec((tm, d), lambda i: (i, 0)),
                  pl.BlockSpec((tm, d), lambda i: (i, 0)),
                  pl.BlockSpec((tm, d), lambda i: (i + n_tiles, 0)),
                  pl.BlockSpec((1, 1, mod_all.shape[-1]), mod_idx),
                  pl.BlockSpec((1, d), lambda i: (0, 0))],
        out_specs=pl.BlockSpec((tm, d), lambda i: (i, 0)),
        out_shape=jax.ShapeDtypeStruct((n_rows, d), F32),
        compiler_params=_cparams("arbitrary"),
        name="combine",
    )(x_new, ys, ys, mod_all, final_g)


def _rope_tables(t_len, rot_dim, pad_rows):
    n_freq = rot_dim // 4
    half = rot_dim // 2
    inv_freq = ROPE_BASE ** (-jnp.arange(n_freq, dtype=F32) / n_freq)
    pos = jnp.arange(t_len)
    row = (pos // GRID_W).astype(F32)
    col = (pos % GRID_W).astype(F32)
    ang = jnp.concatenate([row[:, None] * inv_freq, col[:, None] * inv_freq], axis=-1)
    cos, sin = jnp.cos(ang), jnp.sin(ang)
    reps = GROUP_WIDTH // rot_dim
    zeros = jnp.zeros_like(sin)
    cos_f = jnp.tile(jnp.concatenate([cos, cos], -1), (1, reps))
    sin_a = jnp.tile(jnp.concatenate([-sin, zeros], -1), (1, reps))
    sin_b = jnp.tile(jnp.concatenate([zeros, sin], -1), (1, reps))
    pad1 = jnp.ones((pad_rows, GROUP_WIDTH), F32)
    pad0 = jnp.zeros((pad_rows, GROUP_WIDTH), F32)
    return (jnp.concatenate([cos_f, pad1], 0), jnp.concatenate([sin_a, pad0], 0),
            jnp.concatenate([sin_b, pad0], 0))


def kernel(x, c, ctx, c_ctx, w_mod, b_mod, norm1_g, norm2_g, w_in, ret_decay_logit, ret_norm_g,
           gqa_qnorm_g, gqa_knorm_g, hgrn_lb_logit, hgrn_norm_g, diff_lambda, diff_norm_g, w_out,
           router_w, router_bias, moe_w_gate, moe_w_up, moe_w_down, final_norm_g):
    n_batch, t_len, d = x.shape
    ctx_len = ctx.shape[1]
    depth = w_mod.shape[0]
    gw = GROUP_WIDTH
    lat_rows = n_batch * t_len
    all_rows = lat_rows + n_batch * ctx_len
    assert t_len % ROW_TILE == 0 and ctx_len % ROW_TILE == 0
    assert all(t_len % tq == 0 for tq in ATT_Q_TILE.values())
    assert t_len % OUT_TILE == 0 and (n_batch * ctx_len) % OUT_TILE == 0
    assert ctx_len % REC_CHUNK == 0 and (ctx_len + t_len) % ctx_len == 0 and n_batch + 1 <= 8

    x_all = jnp.concatenate([x.reshape(lat_rows, d), ctx.reshape(n_batch * ctx_len, d)], axis=0)
    cvec = jnp.zeros((8, d), F32).at[:n_batch].set(c).at[n_batch].set(c_ctx)

    tabs = _rope_tables(t_len, HEAD_DIM, ROW_TILE) + _rope_tables(t_len, DIFF_QK_DIM, ROW_TILE)
    head_of = jnp.arange(gw) // HEAD_DIM
    same_head = head_of[:, None] == head_of[None, :]
    seg = same_head.astype(F32) / HEAD_DIM
    bd = same_head.astype(BF16)

    sm = jax.nn.softmax(hgrn_lb_logit.astype(F32), axis=1)
    lower_bounds = jnp.cumsum(sm, axis=1) - sm[:, :1]
    log_gamma = jax.nn.log_sigmoid(ret_decay_logit.astype(F32))

    perm = (jnp.arange(N_EXPERTS) % N_EXPERT_GROUPS) * EXPERTS_PER_GROUP + jnp.arange(N_EXPERTS) // N_EXPERT_GROUPS
    rw_t = router_w.T[perm]
    rbias = jnp.broadcast_to(router_bias.astype(F32)[perm][:, None], (N_EXPERTS, 128))

    tile = lambda g: jnp.tile(g.astype(F32), GROUP_HEADS).reshape(1, gw)

    out = None
    for layer in range(depth):
        need_ctx = layer < depth - 1
        lambda_init = 0.8 - 0.6 * math.exp(-0.3 * layer)
        mod_all = _mod_call(cvec, w_mod[layer], b_mod[layer]).reshape(8, 1, 6 * d)

        w_l = w_in[layer]
        w_bf = w_l.astype(BF16)
        wvt_bf = jnp.concatenate([w_l[:, 2 * gw:3 * gw], w_l[:, 9 * gw:10 * gw]], axis=1).T.astype(BF16)
        zr, zg, zh, zd, vt_all, kg, vg, kd, vd = _in_proj_call(
            x_all, mod_all, norm1_g[layer].reshape(1, d), w_bf, wvt_bf, tabs,
            tile(gqa_qnorm_g[layer]), tile(gqa_knorm_g[layer])[:, :128], seg, lat_rows, t_len)

        rec_kw = dict(n_batch=n_batch, t_len=t_len, ctx_len=ctx_len)
        ret_par = jnp.repeat(log_gamma[layer], HEAD_DIM, axis=-1)
        rf, rb = _rec_call(zr, vt_all, ret_par, bd, hgrn=False, cols=(0, 1, 1, 2), vt_row=0, **rec_kw)
        hf, hb = _rec_call(zh, vt_all, lower_bounds[:, layer], bd, hgrn=True, cols=(0, 1, 2, 3),
                           vt_row=1, **rec_kw)

        lp = diff_lambda[layer].astype(F32)
        lam = (jnp.exp(jnp.sum(lp[0] * lp[1])) - jnp.exp(jnp.sum(lp[2] * lp[3])) + lambda_init)
        lam = lam.reshape(1, 1)
        att_kw = dict(n_batch=n_batch, t_len=t_len, ctx_len=ctx_len)
        diff_kw = dict(lam=lam, norm_g=tile(diff_norm_g[layer]), seg=seg, out_scale=1.0 - lambda_init)
        o_gqa = _attn_call("gqa", zg, (kg, vg), ctx_queries=False, **att_kw)
        o_df = _attn_call("diff", zd, (kd, vd), ctx_queries=False, **att_kw, **diff_kw)
        n_rows = lat_rows
        if need_ctx:
            o_gqa = jnp.concatenate(
                [o_gqa, _attn_call("gqa", zg, (kg, vg), ctx_queries=True, **att_kw)], 0)
            o_df = jnp.concatenate(
                [o_df, _attn_call("diff", zd, (kd, vd), ctx_queries=True, **att_kw, **diff_kw)], 0)
            n_rows = all_rows

        x_new, h2, e_t, g_t = _out_proj_call(
            rf, rb, zr, o_gqa, hf, hb, zh, o_df, x_all, mod_all, w_out[layer].astype(BF16),
            tile(ret_norm_g[layer]), tile(hgrn_norm_g[layer]), seg, norm2_g[layer].reshape(1, d),
            rw_t, rbias, n_rows, lat_rows, t_len)

        plan = _dispatch_plan(e_t, g_t, n_rows)
        ys = _expert_call(*plan, h2, moe_w_gate, moe_w_up, moe_w_down, n_rows, layer)
        last = layer == depth - 1
        x_next = _combine_call(x_new, ys, mod_all, final_norm_g.reshape(1, d), n_rows, lat_rows, t_len,
                               final_norm=last)
        if last:
            out = x_next[:lat_rows].reshape(n_batch, t_len, d)
        else:
            x_all = x_next
    return out
```
